```python
import jax, jax.numpy as jnp
from jax import lax
import numpy as np

D_MODEL = 1024
BATCH = 4
SEQ = 8192
DEPTH = 4

CHUNK = 64
CONV_W = 512
CONV_K = 31
SGU_W = 512
SGU_GROUPS = 4
SGU_CHUNK = 128
ATTN_HEADS = 8
ATTN_HEAD_DIM = 64
ATTN_W = ATTN_HEADS * ATTN_HEAD_DIM
ATTN_LEFT_CHUNKS = 8
ATTN_BAND = (ATTN_LEFT_CHUNKS + 1) * CHUNK
REL_MIN = -(CHUNK - 1)
REL_MAX = 128
N_REL = REL_MAX - REL_MIN + 1
N_BRANCH = 3
N_IN = 2 * CONV_W + 2 * SGU_W + 3 * ATTN_W + N_BRANCH * D_MODEL
D_FF = 2816
N_EXPERTS = 8
TOP_K = 2
D_FF_EXPERT = 3584
N_DENSE = (DEPTH + 1) // 2
N_MOE = DEPTH // 2
EPS = 1e-6
NEG_INF = -1e30

kernel_name = "hybrid_conv_sgu_chunkattn_moe_trunk"


def rms_norm(x, g):
    xf = x.astype(jnp.float32)
    y = xf * lax.rsqrt(jnp.mean(xf * xf, axis=-1, keepdims=True) + EPS)
    return (y * g.astype(jnp.float32)).astype(x.dtype)


def layer_norm(x, g, b):
    xf = x.astype(jnp.float32)
    mu = jnp.mean(xf, axis=-1, keepdims=True)
    var = jnp.mean(jnp.square(xf - mu), axis=-1, keepdims=True)
    y = (xf - mu) * lax.rsqrt(var + EPS) * g.astype(jnp.float32) + b.astype(jnp.float32)
    return y.astype(x.dtype)


def conv_branch(a, dw_w, dw_b, ln_g, ln_b, out_w):
    h = a[..., :CONV_W] * jax.nn.sigmoid(a[..., CONV_W:])
    h = lax.conv_general_dilated(
        h, dw_w[:, None, :], window_strides=(1,), padding=[(CONV_K - 1, 0)],
        dimension_numbers=("NWC", "WIO", "NWC"), feature_group_count=CONV_W) + dw_b
    h = jax.nn.silu(layer_norm(h, ln_g, ln_b))
    return h @ out_w


def sgu_branch(a, ws, bs, ln_g, ln_b, out_w):
    b, s, _ = a.shape
    a = jax.nn.gelu(a, approximate=False)
    u, v = a[..., :SGU_W], a[..., SGU_W:]
    v = layer_norm(v, ln_g, ln_b)
    n = s // SGU_CHUNK
    v = v.reshape(b, n, SGU_CHUNK, SGU_GROUPS, SGU_W // SGU_GROUPS)
    pos = jnp.arange(SGU_CHUNK)
    mask = (pos[None, :] // CHUNK) <= (pos[:, None] // CHUNK)
    w = jnp.where(mask[None], ws, jnp.zeros_like(ws))
    mixed = jnp.einsum("gts,bnsgc->bntgc", w, v) + bs.T[:, :, None]
    return (u * mixed.reshape(b, s, SGU_W)) @ out_w


def attn_branch(a, rel_table, out_w):
    b, s, _ = a.shape
    nc = s // CHUNK
    shp = (b, nc, CHUNK, ATTN_HEADS, ATTN_HEAD_DIM)
    q = a[..., :ATTN_W].reshape(shp)
    k = a[..., ATTN_W:2 * ATTN_W].reshape(shp)
    v = a[..., 2 * ATTN_W:].reshape(shp)
    pad = ((0, 0), (ATTN_LEFT_CHUNKS, 0), (0, 0), (0, 0), (0, 0))
    kp, vp = jnp.pad(k, pad), jnp.pad(v, pad)
    k_band = jnp.concatenate([kp[:, j:j + nc] for j in range(ATTN_LEFT_CHUNKS + 1)], axis=2)
    v_band = jnp.concatenate([vp[:, j:j + nc] for j in range(ATTN_LEFT_CHUNKS + 1)], axis=2)
    scores = jnp.einsum("bnqhd,bnkhd->bnhqk", q, k_band,
                        preferred_element_type=jnp.float32) * (ATTN_HEAD_DIM ** -0.5)
    qi = jnp.arange(CHUNK)
    kj = jnp.arange(ATTN_BAND)
    rel = ATTN_LEFT_CHUNKS * CHUNK + qi[:, None] - kj[None, :]
    idx = jnp.clip(rel, REL_MIN, REL_MAX) - REL_MIN
    bias = rel_table.astype(jnp.float32)[:, idx]
    key_chunk = jnp.arange(nc)[:, None] - ATTN_LEFT_CHUNKS + kj[None, :] // CHUNK
    valid = key_chunk >= 0
    add = jnp.where(valid[:, None, None, :], bias[None], NEG_INF)
    p = jax.nn.softmax(scores + add[None], axis=-1).astype(v.dtype)
    o = jnp.einsum("bnhqk,bnkhd->bnqhd", p, v_band)
    return o.reshape(b, s, ATTN_W) @ out_w


def swiglu(x, wg, wu, wd):
    return (jax.nn.silu(x @ wg) * (x @ wu)) @ wd


def moe_swiglu(x, router, wg, wu, wd):
    b, s, d = x.shape
    t = x.reshape(b * s, d)
    logits = jnp.dot(t, router, preferred_element_type=jnp.float32)
    top_v, top_i = lax.top_k(logits, TOP_K)
    top_w = jax.nn.softmax(top_v, axis=-1)
    combine = jnp.sum(jax.nn.one_hot(top_i, N_EXPERTS, dtype=jnp.float32) * top_w[..., None], axis=1)
    combine = combine.astype(t.dtype)
    out = jnp.zeros_like(t)
    for e in range(N_EXPERTS):
        out = out + combine[:, e:e + 1] * swiglu(t, wg[e], wu[e], wd[e])
    return out.reshape(b, s, d)


def setup_inputs(seed: int = 0) -> dict:
    key = jax.random.key(seed)
    ks = iter(jax.random.split(key, 32))

    def nrm(shape, scale):
        return jax.random.normal(next(ks), shape, jnp.float32) * scale

    L, D = DEPTH, D_MODEL
    return {
        "x": nrm((BATCH, SEQ, D), 1.0),
        "mix_norm_g": 1.0 + nrm((L, D), 0.02),
        "w_in": nrm((L, D, N_IN), D ** -0.5),
        "b_gate": nrm((L, N_BRANCH * D), 0.02),
        "conv_dw_w": nrm((L, CONV_K, CONV_W), CONV_K ** -0.5),
        "conv_dw_b": nrm((L, CONV_W), 0.02),
        "conv_ln_g": 1.0 + nrm((L, CONV_W), 0.02),
        "conv_ln_b": nrm((L, CONV_W), 0.02),
        "conv_out_w": nrm((L, CONV_W, D), CONV_W ** -0.5),
        "sgu_ln_g": 1.0 + nrm((L, SGU_W), 0.02),
        "sgu_ln_b": nrm((L, SGU_W), 0.02),
        "sgu_ws": nrm((L, SGU_GROUPS, SGU_CHUNK, SGU_CHUNK), SGU_CHUNK ** -0.5),
        "sgu_bs": 1.0 + nrm((L, SGU_GROUPS, SGU_CHUNK), 0.02),
        "sgu_out_w": nrm((L, SGU_W, D), SGU_W ** -0.5),
        "attn_rel_bias": nrm((L, ATTN_HEADS, N_REL), 0.1),
        "attn_out_w": nrm((L, ATTN_W, D), ATTN_W ** -0.5),
        "w_out": nrm((L, D, D), D ** -0.5),
        "ffn_norm_g": 1.0 + nrm((L, D), 0.02),
        "ffn_w_gate": nrm((N_DENSE, D, D_FF), D ** -0.5),
        "ffn_w_up": nrm((N_DENSE, D, D_FF), D ** -0.5),
        "ffn_w_down": nrm((N_DENSE, D_FF, D), D_FF ** -0.5),
        "moe_router": nrm((N_MOE, D, N_EXPERTS), D ** -0.5),
        "moe_w_gate": nrm((N_MOE, N_EXPERTS, D, D_FF_EXPERT), D ** -0.5),
        "moe_w_up": nrm((N_MOE, N_EXPERTS, D, D_FF_EXPERT), D ** -0.5),
        "moe_w_down": nrm((N_MOE, N_EXPERTS, D_FF_EXPERT, D), D_FF_EXPERT ** -0.5),
        "final_norm_g": 1.0 + nrm((D,), 0.02),
    }


def reference(x, mix_norm_g, w_in, b_gate, conv_dw_w, conv_dw_b, conv_ln_g, conv_ln_b,
              conv_out_w, sgu_ln_g, sgu_ln_b, sgu_ws, sgu_bs, sgu_out_w, attn_rel_bias,
              attn_out_w, w_out, ffn_norm_g, ffn_w_gate, ffn_w_up, ffn_w_down, moe_router,
              moe_w_gate, moe_w_up, moe_w_down, final_norm_g):
    o1 = 2 * CONV_W
    o2 = o1 + 2 * SGU_W
    o3 = o2 + 3 * ATTN_W
    h = x
    for layer in range(DEPTH):
        xn = rms_norm(h, mix_norm_g[layer])
        a = xn @ w_in[layer]
        y_conv = conv_branch(a[..., :o1], conv_dw_w[layer], conv_dw_b[layer],
                             conv_ln_g[layer], conv_ln_b[layer], conv_out_w[layer])
        y_sgu = sgu_branch(a[..., o1:o2], sgu_ws[layer], sgu_bs[layer],
                           sgu_ln_g[layer], sgu_ln_b[layer], sgu_out_w[layer])
        y_attn = attn_branch(a[..., o2:o3], attn_rel_bias[layer], attn_out_w[layer])
        g = jax.nn.sigmoid(a[..., o3:] + b_gate[layer])
        merged = (g[..., :D_MODEL] * y_conv
                  + g[..., D_MODEL:2 * D_MODEL] * y_sgu
                  + g[..., 2 * D_MODEL:] * y_attn)
        h = h + merged @ w_out[layer]
        xn = rms_norm(h, ffn_norm_g[layer])
        i = layer // 2
        if layer % 2 == 0:
            h = h + swiglu(xn, ffn_w_gate[i], ffn_w_up[i], ffn_w_down[i])
        else:
            h = h + moe_swiglu(xn, moe_router[i], moe_w_gate[i], moe_w_up[i], moe_w_down[i])
    return rms_norm(h, final_norm_g)
```

```python
import functools

import jax
import jax.numpy as jnp
from jax import lax
from jax.experimental import pallas as pl
from jax.experimental.pallas import tpu as pltpu

D_MODEL = 1024
BATCH = 4
SEQ = 8192
DEPTH = 4
CHUNK = 64
CONV_W = 512
CONV_K = 31
SGU_W = 512
SGU_GROUPS = 4
SGU_CHUNK = 128
SGU_GW = SGU_W // SGU_GROUPS
ATTN_HEADS = 8
ATTN_HEAD_DIM = 64
ATTN_W = ATTN_HEADS * ATTN_HEAD_DIM
ATTN_LEFT_CHUNKS = 8
REL_MIN = -(CHUNK - 1)
REL_MAX = 128
N_BRANCH = 3
N_GATE = N_BRANCH * D_MODEL
N_MIX = 2 * CONV_W + 2 * SGU_W + 3 * ATTN_W
N_IN = N_MIX + N_GATE
D_FF = 2816
N_EXPERTS = 8
TOP_K = 2
D_FF_EXPERT = 3584
EPS = 1e-6
NEG_INF = -1e30

T = BATCH * SEQ
F32 = jnp.float32
BF16 = jnp.bfloat16

COL_CONV = N_GATE
COL_SGU = COL_CONV + 2 * CONV_W
COL_Q = COL_SGU + 2 * SGU_W
COL_K = COL_Q + ATTN_W
COL_V = COL_K + ATTN_W

TM_IN = 1024
TN_IN = 512
TS_CONV = 512
CONV_HALO = 32
CONV_ROWS = 64
TS_SGU = 512
TQ_ATTN = 256
ATTN_PREV = ATTN_LEFT_CHUNKS * CHUNK
ATTN_KWIN = ATTN_PREV + TQ_ATTN
TM_MERGE = 512
TM_FFN = 1024
TF_FFN = 256
TM_MOE = 512
TF_MOE = 512
NT_MOE = (TOP_K * T) // TM_MOE + N_EXPERTS
P_MOE = NT_MOE * TM_MOE
ROUTE_LANES = 128

VMEM_LIMIT = 56 * 1024 * 1024


def _params(*sem):
    return pltpu.CompilerParams(dimension_semantics=sem, vmem_limit_bytes=VMEM_LIMIT)


def _rms(x, g):
    return x * lax.rsqrt(jnp.mean(x * x, axis=-1, keepdims=True) + EPS) * g


def _layer_norm(x, g, b):
    mu = jnp.mean(x, axis=-1, keepdims=True)
    xc = x - mu
    var = jnp.mean(xc * xc, axis=-1, keepdims=True)
    return xc * lax.rsqrt(var + EPS) * g + b


def _sigmoid(x):
    return 1.0 / (1.0 + jnp.exp(-x))


def _in_proj_kernel(h_ref, g_ref, w_ref, o_ref, xn_ref):
    @pl.when(pl.program_id(1) == 0)
    def _():
        xn_ref[...] = _rms(h_ref[...], g_ref[...]).astype(BF16)

    o_ref[...] = jnp.dot(xn_ref[...], w_ref[...],
                         preferred_element_type=F32).astype(BF16)


def _in_proj(h, g, w):
    return pl.pallas_call(
        _in_proj_kernel,
        grid=(T // TM_IN, N_IN // TN_IN),
        in_specs=[
            pl.BlockSpec((TM_IN, D_MODEL), lambda i, j: (i, 0)),
            pl.BlockSpec((1, D_MODEL), lambda i, j: (0, 0)),
            pl.BlockSpec((D_MODEL, TN_IN), lambda i, j: (0, j)),
        ],
        out_specs=pl.BlockSpec((TM_IN, TN_IN), lambda i, j: (i, j)),
        out_shape=jax.ShapeDtypeStruct((T, N_IN), BF16),
        scratch_shapes=[pltpu.VMEM((TM_IN, D_MODEL), BF16)],
        compiler_params=_params("parallel", "arbitrary"),
        name="in_proj",
    )(h, g, w)


def _conv_kernel(x_ref, halo_ref, w_ref, b_ref, lg_ref, lb_ref, o_ref, buf_ref):
    def glu(x):
        x = x.astype(F32)
        return x[:, :CONV_W] * _sigmoid(x[:, CONV_W:])

    seq_start = (pl.program_id(0) % (SEQ // TS_CONV)) == 0
    buf_ref[0:CONV_HALO, :] = jnp.where(seq_start, 0.0, glu(halo_ref[...]))
    buf_ref[CONV_HALO:, :] = glu(x_ref[...])

    first_tap = CONV_HALO - (CONV_K - 1)
    for r0 in range(0, TS_CONV, CONV_ROWS):
        acc = jnp.broadcast_to(b_ref[...], (CONV_ROWS, CONV_W))
        for k in range(CONV_K):
            acc = acc + w_ref[k:k + 1, :] * buf_ref[pl.ds(r0 + first_tap + k, CONV_ROWS), :]
        y = _layer_norm(acc, lg_ref[...], lb_ref[...])
        o_ref[pl.ds(r0, CONV_ROWS), :] = (y * _sigmoid(y)).astype(BF16)


def _conv_branch(a, dw_w, dw_b, ln_g, ln_b):
    halo_per_tile = TS_CONV // CONV_HALO
    col = COL_CONV // (2 * CONV_W)
    vec = pl.BlockSpec((1, CONV_W), lambda i: (0, 0))
    return pl.pallas_call(
        _conv_kernel,
        grid=(T // TS_CONV,),
        in_specs=[
            pl.BlockSpec((TS_CONV, 2 * CONV_W), lambda i: (i, col)),
            pl.BlockSpec((CONV_HALO, 2 * CONV_W),
                         lambda i: (jnp.maximum(i * halo_per_tile - 1, 0), col)),
            pl.BlockSpec((CONV_K, CONV_W), lambda i: (0, 0)),
            vec, vec, vec,
        ],
        out_specs=pl.BlockSpec((TS_CONV, CONV_W), lambda i: (i, 0)),
        out_shape=jax.ShapeDtypeStruct((T, CONV_W), BF16),
        scratch_shapes=[pltpu.VMEM((CONV_HALO + TS_CONV, CONV_W), F32)],
        compiler_params=_params("parallel"),
        name="conv_branch",
    )(a, a, dw_w, dw_b, ln_g, ln_b)


def _sgu_kernel(x_ref, lg_ref, lb_ref, w_ref, bias_ref, o_ref):
    x = x_ref[...].astype(F32)
    x = 0.5 * x * (1.0 + lax.erf(x * (0.5 ** 0.5)))
    u = x[:, :SGU_W]
    v = _layer_norm(x[:, SGU_W:], lg_ref[...], lb_ref[...]).astype(BF16)

    t_half = lax.broadcasted_iota(jnp.int32, (SGU_CHUNK, SGU_CHUNK), 0) // CHUNK
    s_half = lax.broadcasted_iota(jnp.int32, (SGU_CHUNK, SGU_CHUNK), 1) // CHUNK
    causal = s_half <= t_half
    n_chunks = TS_SGU // SGU_CHUNK

    mixed = []
    for g in range(SGU_GROUPS):
        w = jnp.where(causal, w_ref[g], 0.0).astype(BF16)
        rhs = jnp.concatenate(
            [v[n * SGU_CHUNK:(n + 1) * SGU_CHUNK, g * SGU_GW:(g + 1) * SGU_GW]
             for n in range(n_chunks)], axis=1)
        mixed.append(jnp.dot(w, rhs, preferred_element_type=F32))
    for n in range(n_chunks):
        m = jnp.concatenate(
            [mixed[g][:, n * SGU_GW:(n + 1) * SGU_GW] for g in range(SGU_GROUPS)], axis=1)
        rows = slice(n * SGU_CHUNK, (n + 1) * SGU_CHUNK)
        o_ref[rows, :] = (u[rows, :] * (m + bias_ref[...])).astype(BF16)


def _sgu_branch(a, ln_g, ln_b, ws, bias):
    col = COL_SGU // (2 * SGU_W)
    vec = pl.BlockSpec((1, SGU_W), lambda i: (0, 0))
    return pl.pallas_call(
        _sgu_kernel,
        grid=(T // TS_SGU,),
        in_specs=[
            pl.BlockSpec((TS_SGU, 2 * SGU_W), lambda i: (i, col)),
            vec, vec,
            pl.BlockSpec((SGU_GROUPS, SGU_CHUNK, SGU_CHUNK), lambda i: (0, 0, 0)),
            pl.BlockSpec((SGU_CHUNK, SGU_W), lambda i: (0, 0)),
        ],
        out_specs=pl.BlockSpec((TS_SGU, SGU_W), lambda i: (i, 0)),
        out_shape=jax.ShapeDtypeStruct((T, SGU_W), BF16),
        compiler_params=_params("parallel"),
        name="sgu_branch",
    )(a, ln_g, ln_b, ws, bias)


def _attn_kernel(q_ref, k0_ref, k1_ref, k2_ref, v0_ref, v1_ref, v2_ref, bias_ref, o_ref):
    i = pl.program_id(1)
    q = q_ref[...] * (ATTN_HEAD_DIM ** -0.5)
    kwin = jnp.concatenate([k0_ref[...], k1_ref[...], k2_ref[...]], axis=0)
    vwin = jnp.concatenate([v0_ref[...], v1_ref[...], v2_ref[...]], axis=0)
    key_pos = lax.broadcasted_iota(jnp.int32, (1, ATTN_KWIN), 1) + (i * TQ_ATTN - ATTN_PREV)
    start_mask = jnp.where(key_pos >= 0, 0.0, NEG_INF)

    outs = []
    for h in range(ATTN_HEADS):
        cols = slice(h * ATTN_HEAD_DIM, (h + 1) * ATTN_HEAD_DIM)
        s = lax.dot_general(q[:, cols], kwin[:, cols], (((1,), (1,)), ((), ())),
                            preferred_element_type=F32)
        s = s + bias_ref[h] + start_mask
        p = jnp.exp(s - jnp.max(s, axis=-1, keepdims=True))
        denom = jnp.sum(p, axis=-1, keepdims=True)
        o = jnp.dot(p.astype(BF16), vwin[:, cols], preferred_element_type=F32)
        outs.append(o / denom)
    o_ref[...] = jnp.concatenate(outs, axis=1).astype(BF16)


def _attn_bias(rel_table):
    qi = jnp.arange(TQ_ATTN)[:, None] + ATTN_PREV
    kj = jnp.arange(ATTN_KWIN)[None, :]
    idx = jnp.clip(qi - kj, REL_MIN, REL_MAX) - REL_MIN
    qc, kc = qi // CHUNK, kj // CHUNK
    band = (kc <= qc) & (kc >= qc - ATTN_LEFT_CHUNKS)
    return jnp.where(band[None], rel_table.astype(F32)[:, idx], NEG_INF)


def _attn_branch(a, bias):
    nq = SEQ // TQ_ATTN
    prev_tiles = ATTN_PREV // TQ_ATTN

    def kv_spec(col, back):
        return pl.BlockSpec(
            (TQ_ATTN, ATTN_W),
            lambda b, i: (b * nq + jnp.maximum(i - back, 0), col))

    qcol, kcol, vcol = COL_Q // ATTN_W, COL_K // ATTN_W, COL_V // ATTN_W
    return pl.pallas_call(
        _attn_kernel,
        grid=(BATCH, nq),
        in_specs=[pl.BlockSpec((TQ_ATTN, ATTN_W), lambda b, i: (b * nq + i, qcol))]
        + [kv_spec(kcol, back) for back in range(prev_tiles, -1, -1)]
        + [kv_spec(vcol, back) for back in range(prev_tiles, -1, -1)]
        + [pl.BlockSpec((ATTN_HEADS, TQ_ATTN, ATTN_KWIN), lambda b, i: (0, 0, 0))],
        out_specs=pl.BlockSpec((TQ_ATTN, ATTN_W), lambda b, i: (b * nq + i, 0)),
        out_shape=jax.ShapeDtypeStruct((T, ATTN_W), BF16),
        compiler_params=_params("parallel", "parallel"),
        name="attn_branch",
    )(a, a, a, a, a, a, a, bias)


def _merge_kernel(with_router, *refs):
    if with_router:
        (h_ref, xc_ref, xs_ref, xa_ref, gc_ref, gs_ref, ga_ref, bg_ref, wc_ref, ws_ref,
         wa_ref, wo_ref, ng_ref, r_ref, ho_ref, xn_ref, route_ref) = refs
    else:
        (h_ref, xc_ref, xs_ref, xa_ref, gc_ref, gs_ref, ga_ref, bg_ref, wc_ref, ws_ref,
         wa_ref, wo_ref, ng_ref, ho_ref, xn_ref) = refs

    merged = None
    for n, (x_ref, g_ref, w_ref) in enumerate(
            ((xc_ref, gc_ref, wc_ref), (xs_ref, gs_ref, ws_ref), (xa_ref, ga_ref, wa_ref))):
        y = jnp.dot(x_ref[...], w_ref[...], preferred_element_type=F32)
        gate = _sigmoid(g_ref[...].astype(F32) + bg_ref[:, n * D_MODEL:(n + 1) * D_MODEL])
        merged = gate * y if merged is None else merged + gate * y
    h = h_ref[...] + jnp.dot(merged.astype(BF16), wo_ref[...], preferred_element_type=F32)
    ho_ref[...] = h
    xn = _rms(h, ng_ref[...])
    xn_ref[...] = xn.astype(BF16)

    if with_router:
        logits = jnp.dot(xn, r_ref[...], preferred_element_type=F32,
                         precision=lax.Precision.HIGHEST)
        lane = lax.broadcasted_iota(jnp.int32, logits.shape, 1)
        logits = jnp.where(lane < N_EXPERTS, logits, NEG_INF)
        m1 = jnp.max(logits, axis=-1, keepdims=True)
        i1 = jnp.min(jnp.where(logits == m1, lane, ROUTE_LANES), axis=-1, keepdims=True)
        rest = jnp.where(lane == i1, NEG_INF, logits)
        m2 = jnp.max(rest, axis=-1, keepdims=True)
        i2 = jnp.min(jnp.where(rest == m2, lane, ROUTE_LANES), axis=-1, keepdims=True)
        e2 = jnp.exp(m2 - m1)
        w1 = 1.0 / (1.0 + e2)
        w2 = e2 / (1.0 + e2)
        route_ref[...] = jnp.where(
            lane == 0, i1.astype(F32),
            jnp.where(lane == 1, i2.astype(F32),
                      jnp.where(lane == 2, w1, jnp.where(lane == 3, w2, 0.0))))


def _merge(h, xc, xs, xa, a, b_gate, wc, ws, wa, wo, norm_g, router=None):
    with_router = router is not None
    row = lambda w: pl.BlockSpec((TM_MERGE, w), lambda i: (i, 0))
    gate = lambda n: pl.BlockSpec((TM_MERGE, D_MODEL), lambda i: (i, n))
    full = lambda r, c: pl.BlockSpec((r, c), lambda i: (0, 0))
    in_specs = [row(D_MODEL), row(CONV_W), row(SGU_W), row(ATTN_W),
                gate(0), gate(1), gate(2), full(1, N_GATE),
                full(CONV_W, D_MODEL), full(SGU_W, D_MODEL), full(ATTN_W, D_MODEL),
                full(D_MODEL, D_MODEL), full(1, D_MODEL)]
    args = [h, xc, xs, xa, a, a, a, b_gate, wc, ws, wa, wo, norm_g]
    out_specs = [row(D_MODEL), row(D_MODEL)]
    out_shape = [jax.ShapeDtypeStruct((T, D_MODEL), F32),
                 jax.ShapeDtypeStruct((T, D_MODEL), BF16)]
    if with_router:
        in_specs.append(full(D_MODEL, ROUTE_LANES))
        args.append(router)
        out_specs.append(row(ROUTE_LANES))
        out_shape.append(jax.ShapeDtypeStruct((T, ROUTE_LANES), F32))
    return pl.pallas_call(
        functools.partial(_merge_kernel, with_router),
        grid=(T // TM_MERGE,),
        in_specs=in_specs,
        out_specs=out_specs,
        out_shape=out_shape,
        compiler_params=_params("parallel"),
        name="merge_router" if with_router else "merge",
    )(*args)


def _swiglu_step(x, wg_ref, wu_ref, wd_ref):
    g = jnp.dot(x, wg_ref[...], preferred_element_type=F32)
    u = jnp.dot(x, wu_ref[...], preferred_element_type=F32)
    act = (g * _sigmoid(g) * u).astype(BF16)
    return jnp.dot(act, wd_ref[...], preferred_element_type=F32)


def _ffn_kernel(h_ref, x_ref, wg_ref, wu_ref, wd_ref, o_ref, acc_ref):
    j = pl.program_id(1)

    @pl.when(j == 0)
    def _():
        acc_ref[...] = h_ref[...]

    acc_ref[...] += _swiglu_step(x_ref[...], wg_ref, wu_ref, wd_ref)

    @pl.when(j == pl.num_programs(1) - 1)
    def _():
        o_ref[...] = acc_ref[...]


def _ffn(h, xn, wg, wu, wd):
    return pl.pallas_call(
        _ffn_kernel,
        grid=(T // TM_FFN, D_FF // TF_FFN),
        in_specs=[
            pl.BlockSpec((TM_FFN, D_MODEL), lambda i, j: (i, 0)),
            pl.BlockSpec((TM_FFN, D_MODEL), lambda i, j: (i, 0)),
            pl.BlockSpec((D_MODEL, TF_FFN), lambda i, j: (0, j)),
            pl.BlockSpec((D_MODEL, TF_FFN), lambda i, j: (0, j)),
            pl.BlockSpec((TF_FFN, D_MODEL), lambda i, j: (j, 0)),
        ],
        out_specs=pl.BlockSpec((TM_FFN, D_MODEL), lambda i, j: (i, 0)),
        out_shape=jax.ShapeDtypeStruct((T, D_MODEL), F32),
        scratch_shapes=[pltpu.VMEM((TM_FFN, D_MODEL), F32)],
        compiler_params=_params("parallel", "arbitrary"),
        name="ffn",
    )(h, xn, wg, wu, wd)


def _moe_kernel(tile_expert_ref, n_tiles_ref, x_ref, cw_ref, wg_ref, wu_ref, wd_ref,
                o_ref, acc_ref):
    i, j = pl.program_id(0), pl.program_id(1)

    @pl.when(i < n_tiles_ref[0])
    def _():
        y = _swiglu_step(x_ref[...], wg_ref.at[0], wu_ref.at[0], wd_ref.at[0])

        @pl.when(j == 0)
        def _():
            acc_ref[...] = y

        @pl.when(j > 0)
        def _():
            acc_ref[...] += y

        @pl.when(j == pl.num_programs(1) - 1)
        def _():
            o_ref[...] = (acc_ref[...] * cw_ref[...]).astype(BF16)


def _moe(tile_expert, n_tiles, xs, cw, wg, wu, wd):
    nj = D_FF_EXPERT // TF_MOE

    def jj(i, j, nt):
        return jnp.where(i < nt[0], j, nj - 1)

    grid_spec = pltpu.PrefetchScalarGridSpec(
        num_scalar_prefetch=2,
        grid=(NT_MOE, nj),
        in_specs=[
            pl.BlockSpec((TM_MOE, D_MODEL), lambda i, j, te, nt: (i, 0)),
            pl.BlockSpec((TM_MOE, 1), lambda i, j, te, nt: (i, 0)),
            pl.BlockSpec((1, D_MODEL, TF_MOE), lambda i, j, te, nt: (te[i], 0, jj(i, j, nt))),
            pl.BlockSpec((1, D_MODEL, TF_MOE), lambda i, j, te, nt: (te[i], 0, jj(i, j, nt))),
            pl.BlockSpec((1, TF_MOE, D_MODEL), lambda i, j, te, nt: (te[i], jj(i, j, nt), 0)),
        ],
        out_specs=pl.BlockSpec((TM_MOE, D_MODEL), lambda i, j, te, nt: (i, 0)),
        scratch_shapes=[pltpu.VMEM((TM_MOE, D_MODEL), F32)],
    )
    return pl.pallas_call(
        _moe_kernel,
        grid_spec=grid_spec,
        out_shape=jax.ShapeDtypeStruct((P_MOE, D_MODEL), BF16),
        compiler_params=_params("arbitrary", "arbitrary"),
        name="moe_experts",
    )(tile_expert, n_tiles, xs, cw, wg, wu, wd)


def _route(route):
    experts = jnp.concatenate([route[:, 0], route[:, 1]]).astype(jnp.int32)
    weights = jnp.concatenate([route[:, 2], route[:, 3]])
    onehot = (experts[:, None] == jnp.arange(N_EXPERTS)[None, :]).astype(jnp.int32)
    rank = jnp.sum((jnp.cumsum(onehot, axis=0) - 1) * onehot, axis=1)
    counts = jnp.sum(onehot, axis=0)
    tiles = (counts + TM_MOE - 1) // TM_MOE
    tile_end = jnp.cumsum(tiles)
    row_start = (tile_end - tiles) * TM_MOE
    pos = row_start[experts] + rank
    n_tiles = tile_end[-1:]
    tile_expert = jnp.minimum(
        jnp.sum(jnp.arange(NT_MOE)[:, None] >= tile_end[None, :], axis=1), N_EXPERTS - 1)
    order = jnp.argsort(experts, stable=True)
    sorted_start = jnp.cumsum(counts) - counts
    row = jnp.arange(P_MOE)
    row_expert = tile_expert[row // TM_MOE]
    within = row - row_start[row_expert]
    live = within < counts[row_expert]
    src = order[jnp.where(live, sorted_start[row_expert] + within, 0)]
    row_token = jnp.where(live, src % T, 0)
    row_weight = jnp.where(live, weights[src], 0.0)
    return (tile_expert.astype(jnp.int32), n_tiles.astype(jnp.int32), row_token,
            row_weight[:, None], pos[:T], pos[T:])


def _final_norm_kernel(h_ref, g_ref, o_ref):
    o_ref[...] = _rms(h_ref[...], g_ref[...])


def _final_norm(h, g):
    return pl.pallas_call(
        _final_norm_kernel,
        grid=(T // TM_IN,),
        in_specs=[pl.BlockSpec((TM_IN, D_MODEL), lambda i: (i, 0)),
                  pl.BlockSpec((1, D_MODEL), lambda i: (0, 0))],
        out_specs=pl.BlockSpec((TM_IN, D_MODEL), lambda i: (i, 0)),
        out_shape=jax.ShapeDtypeStruct((T, D_MODEL), F32),
        compiler_params=_params("parallel"),
        name="final_norm",
    )(h, g)


def kernel(x, mix_norm_g, w_in, b_gate, conv_dw_w, conv_dw_b, conv_ln_g, conv_ln_b,
           conv_out_w, sgu_ln_g, sgu_ln_b, sgu_ws, sgu_bs, sgu_out_w, attn_rel_bias,
           attn_out_w, w_out, ffn_norm_g, ffn_w_gate, ffn_w_up, ffn_w_down, moe_router,
           moe_w_gate, moe_w_up, moe_w_down, final_norm_g):
    h = x.reshape(T, D_MODEL)
    for layer in range(DEPTH):
        w_in_l = jnp.concatenate(
            [w_in[layer][:, N_MIX:], w_in[layer][:, :N_MIX]], axis=1).astype(BF16)
        a = _in_proj(h, mix_norm_g[layer][None], w_in_l)
        xc = _conv_branch(a, conv_dw_w[layer], conv_dw_b[layer][None],
                          conv_ln_g[layer][None], conv_ln_b[layer][None])
        sgu_bias = jnp.repeat(sgu_bs[layer].T, SGU_GW, axis=1)
        xs = _sgu_branch(a, sgu_ln_g[layer][None], sgu_ln_b[layer][None],
                         sgu_ws[layer], sgu_bias)
        xa = _attn_branch(a, _attn_bias(attn_rel_bias[layer]))

        i = layer // 2
        router = None
        if layer % 2 == 1:
            router = jnp.pad(moe_router[i], ((0, 0), (0, ROUTE_LANES - N_EXPERTS)))
        outs = _merge(h, xc, xs, xa, a, b_gate[layer][None],
                      conv_out_w[layer].astype(BF16), sgu_out_w[layer].astype(BF16),
                      attn_out_w[layer].astype(BF16), w_out[layer].astype(BF16),
                      ffn_norm_g[layer][None], router)
        if layer % 2 == 0:
            h, xn = outs
            h = _ffn(h, xn, ffn_w_gate[i].astype(BF16), ffn_w_up[i].astype(BF16),
                     ffn_w_down[i].astype(BF16))
        else:
            h, xn, route = outs
            tile_expert, n_tiles, row_token, row_weight, pos1, pos2 = _route(route)
            ys = _moe(tile_expert, n_tiles, jnp.take(xn, row_token, axis=0), row_weight,
                      moe_w_gate[i].astype(BF16), moe_w_up[i].astype(BF16),
                      moe_w_down[i].astype(BF16))
            h = (h + jnp.take(ys, pos1, axis=0).astype(F32)
                 + jnp.take(ys, pos2, axis=0).astype(F32))
    return _final_norm(h, final_norm_g[None]).reshape(BATCH, SEQ, D_MODEL)
```

```python
import functools

import jax
import jax.numpy as jnp
from jax import lax
from jax.experimental import pallas as pl
from jax.experimental.pallas import tpu as pltpu

D_MODEL = 1024
BATCH = 4
SEQ = 8192
DEPTH = 4
CHUNK = 64
CONV_W = 512
CONV_K = 31
SGU_W = 512
SGU_GROUPS = 4
SGU_CHUNK = 128
SGU_GW = SGU_W // SGU_GROUPS
ATTN_HEADS = 8
ATTN_HEAD_DIM = 64
ATTN_W = ATTN_HEADS * ATTN_HEAD_DIM
ATTN_LEFT_CHUNKS = 8
REL_MIN = -(CHUNK - 1)
REL_MAX = 128
N_REL = REL_MAX - REL_MIN + 1
N_BRANCH = 3
N_GATE = N_BRANCH * D_MODEL
N_MIX = 2 * CONV_W + 2 * SGU_W + 3 * ATTN_W
N_IN = N_MIX + N_GATE
D_FF = 2816
N_EXPERTS = 8
TOP_K = 2
D_FF_EXPERT = 3584
EPS = 1e-6
NEG_INF = -1e30

T = BATCH * SEQ
F32 = jnp.float32
BF16 = jnp.bfloat16

SUBLANES = 8
LANES = 128

COL_CONV = N_GATE
COL_SGU = COL_CONV + 2 * CONV_W
COL_Q = COL_SGU + 2 * SGU_W
COL_K = COL_Q + ATTN_W
COL_V = COL_K + ATTN_W

TM_NORM = 1024
TM_IN = 512
TN_IN = 512
TS_CONV = 512
CONV_HALO = 32
CONV_ROWS = 64
CONV_FIRST_TAP = CONV_HALO - (CONV_K - 1)
TS_SGU = 512
TQ_ATTN = 256
ATTN_PREV = ATTN_LEFT_CHUNKS * CHUNK
ATTN_BAND = (ATTN_LEFT_CHUNKS + 1) * CHUNK
ATTN_RVEC = 1024
TM_MERGE = 512
TM_FFN = 512
TF_FFN = 256
TM_MOE = 512
TF_MOE = 1792
TC_MOE = 256
NT_MOE = (TOP_K * T) // TM_MOE + N_EXPERTS
P_MOE = NT_MOE * TM_MOE
ROUTE_LANES = LANES

VMEM_LIMIT = 56 * 1024 * 1024


def _params(*sem):
    return pltpu.CompilerParams(dimension_semantics=sem, vmem_limit_bytes=VMEM_LIMIT)


def _resident(shape):
    return pl.BlockSpec(shape, lambda *_: (0,) * len(shape), pipeline_mode=pl.Buffered(1))


def _rms(x, g):
    return x * lax.rsqrt(jnp.mean(x * x, axis=-1, keepdims=True) + EPS) * g


def _layer_norm(x, g, b):
    mu = jnp.mean(x, axis=-1, keepdims=True)
    xc = x - mu
    var = jnp.mean(xc * xc, axis=-1, keepdims=True)
    return xc * lax.rsqrt(var + EPS) * g + b


def _sigmoid(x):
    return 1.0 / (1.0 + jnp.exp(-x))


def _norm_kernel(h_ref, g_ref, o_ref):
    o_ref[...] = _rms(h_ref[...], g_ref[...]).astype(o_ref.dtype)


def _norm(h, g, dtype):
    return pl.pallas_call(
        _norm_kernel,
        grid=(T // TM_NORM,),
        in_specs=[pl.BlockSpec((TM_NORM, D_MODEL), lambda i: (i, 0)),
                  pl.BlockSpec((1, D_MODEL), lambda i: (0, 0))],
        out_specs=pl.BlockSpec((TM_NORM, D_MODEL), lambda i: (i, 0)),
        out_shape=jax.ShapeDtypeStruct((T, D_MODEL), dtype),
        compiler_params=_params("parallel"),
        name="norm",
    )(h, g)


def _in_proj_kernel(x_ref, w_ref, o_ref):
    x = x_ref[...]
    for n in range(N_IN // TN_IN):
        cols = slice(n * TN_IN, (n + 1) * TN_IN)
        o_ref[:, cols] = jnp.dot(x, w_ref[:, cols], preferred_element_type=F32).astype(BF16)


def _in_proj(xn, w):
    return pl.pallas_call(
        _in_proj_kernel,
        grid=(T // TM_IN,),
        in_specs=[pl.BlockSpec((TM_IN, D_MODEL), lambda i: (i, 0)),
                  _resident((D_MODEL, N_IN))],
        out_specs=pl.BlockSpec((TM_IN, N_IN), lambda i: (i, 0)),
        out_shape=jax.ShapeDtypeStruct((T, N_IN), BF16),
        compiler_params=_params("parallel"),
        name="in_proj",
    )(xn, w)


def _conv_kernel(x_ref, halo_ref, w_ref, b_ref, lg_ref, lb_ref, o_ref, buf_ref, shift_ref):
    def glu(x):
        x = x.astype(F32)
        return x[:, :CONV_W] * _sigmoid(x[:, CONV_W:])

    seq_start = (pl.program_id(0) % (SEQ // TS_CONV)) == 0
    buf_ref[0:CONV_HALO, :] = jnp.where(seq_start, 0.0, glu(halo_ref[...]))
    buf_ref[CONV_HALO:, :] = glu(x_ref[...])
    n_shift = shift_ref.shape[1]
    for s in range(1, SUBLANES):
        shift_ref[s] = buf_ref[pl.ds(s, n_shift), :]

    for r0 in range(0, TS_CONV, CONV_ROWS):
        acc = jnp.broadcast_to(b_ref[...], (CONV_ROWS, CONV_W))
        for k in range(CONV_K):
            base, s = divmod(CONV_FIRST_TAP + k, SUBLANES)
            row = r0 + base * SUBLANES
            if s == 0:
                tap = buf_ref[pl.ds(row, CONV_ROWS), :]
            else:
                tap = shift_ref[s, pl.ds(row, CONV_ROWS), :]
            acc = acc + w_ref[k:k + 1, :] * tap
        y = _layer_norm(acc, lg_ref[...], lb_ref[...])
        o_ref[pl.ds(r0, CONV_ROWS), :] = (y * _sigmoid(y)).astype(BF16)


def _conv_branch(a, dw_w, dw_b, ln_g, ln_b):
    halo_per_tile = TS_CONV // CONV_HALO
    col = COL_CONV // (2 * CONV_W)
    vec = pl.BlockSpec((1, CONV_W), lambda i: (0, 0))
    n_buf = CONV_HALO + TS_CONV
    return pl.pallas_call(
        _conv_kernel,
        grid=(T // TS_CONV,),
        in_specs=[
            pl.BlockSpec((TS_CONV, 2 * CONV_W), lambda i: (i, col)),
            pl.BlockSpec((CONV_HALO, 2 * CONV_W),
                         lambda i: (jnp.maximum(i * halo_per_tile - 1, 0), col)),
            pl.BlockSpec((CONV_K, CONV_W), lambda i: (0, 0)),
            vec, vec, vec,
        ],
        out_specs=pl.BlockSpec((TS_CONV, CONV_W), lambda i: (i, 0)),
        out_shape=jax.ShapeDtypeStruct((T, CONV_W), BF16),
        scratch_shapes=[pltpu.VMEM((n_buf, CONV_W), F32),
                        pltpu.VMEM((SUBLANES, n_buf - SUBLANES, CONV_W), F32)],
        compiler_params=_params("parallel"),
        name="conv_branch",
    )(a, a, dw_w, dw_b, ln_g, ln_b)


def _sgu_kernel(x_ref, lg_ref, lb_ref, w_ref, bias_ref, o_ref):
    x = x_ref[...].astype(F32)
    x = 0.5 * x * (1.0 + lax.erf(x * (0.5 ** 0.5)))
    u = x[:, :SGU_W]
    v = _layer_norm(x[:, SGU_W:], lg_ref[...], lb_ref[...]).astype(BF16)

    t_half = lax.broadcasted_iota(jnp.int32, (SGU_CHUNK, SGU_CHUNK), 0) // CHUNK
    s_half = lax.broadcasted_iota(jnp.int32, (SGU_CHUNK, SGU_CHUNK), 1) // CHUNK
    causal = s_half <= t_half
    n_chunks = TS_SGU // SGU_CHUNK

    mixed = []
    for g in range(SGU_GROUPS):
        w = jnp.where(causal, w_ref[g], 0.0).astype(BF16)
        rhs = jnp.concatenate(
            [v[n * SGU_CHUNK:(n + 1) * SGU_CHUNK, g * SGU_GW:(g + 1) * SGU_GW]
             for n in range(n_chunks)], axis=1)
        mixed.append(jnp.dot(w, rhs, preferred_element_type=F32))
    for n in range(n_chunks):
        m = jnp.concatenate(
            [mixed[g][:, n * SGU_GW:(n + 1) * SGU_GW] for g in range(SGU_GROUPS)], axis=1)
        rows = slice(n * SGU_CHUNK, (n + 1) * SGU_CHUNK)
        o_ref[rows, :] = (u[rows, :] * (m + bias_ref[...])).astype(BF16)


def _sgu_branch(a, ln_g, ln_b, ws, bias):
    col = COL_SGU // (2 * SGU_W)
    vec = pl.BlockSpec((1, SGU_W), lambda i: (0, 0))
    return pl.pallas_call(
        _sgu_kernel,
        grid=(T // TS_SGU,),
        in_specs=[
            pl.BlockSpec((TS_SGU, 2 * SGU_W), lambda i: (i, col)),
            vec, vec,
            pl.BlockSpec((SGU_GROUPS, SGU_CHUNK, SGU_CHUNK), lambda i: (0, 0, 0)),
            pl.BlockSpec((SGU_CHUNK, SGU_W), lambda i: (0, 0)),
        ],
        out_specs=pl.BlockSpec((TS_SGU, SGU_W), lambda i: (i, 0)),
        out_shape=jax.ShapeDtypeStruct((T, SGU_W), BF16),
        compiler_params=_params("parallel"),
        name="sgu_branch",
    )(a, ln_g, ln_b, ws, bias)


def _attn_bias_kernel(rvec_ref, o_ref):
    x = jnp.broadcast_to(rvec_ref[0], (CHUNK, ATTN_RVEC))
    y = pltpu.roll(x, ATTN_RVEC - (CHUNK - 1), 1, stride=1, stride_axis=0)
    o_ref[0] = y[:, :ATTN_BAND]


def _attn_bias(rel_table):
    n_hi = ATTN_PREV + CHUNK - 1 - REL_MAX + 1
    n_lo = ATTN_RVEC - n_hi - (N_REL - 1)
    rvec = jnp.concatenate([
        jnp.broadcast_to(rel_table[:, N_REL - 1:], (ATTN_HEADS, n_hi)),
        rel_table[:, N_REL - 2:0:-1],
        jnp.broadcast_to(rel_table[:, :1], (ATTN_HEADS, n_lo + 1)),
    ], axis=1).astype(F32)
    return pl.pallas_call(
        _attn_bias_kernel,
        grid=(ATTN_HEADS,),
        in_specs=[pl.BlockSpec((1, 1, ATTN_RVEC), lambda h: (h, 0, 0))],
        out_specs=pl.BlockSpec((1, CHUNK, ATTN_BAND), lambda h: (h, 0, 0)),
        out_shape=jax.ShapeDtypeStruct((ATTN_HEADS, CHUNK, ATTN_BAND), F32),
        compiler_params=_params("parallel"),
        name="attn_bias",
    )(rvec[:, None, :])


def _attn_kernel(q_ref, k0_ref, k1_ref, k2_ref, v0_ref, v1_ref, v2_ref, bias_ref, o_ref):
    i = pl.program_id(1)
    q = q_ref[...] * (ATTN_HEAD_DIM ** -0.5)
    kwin = jnp.concatenate([k0_ref[...], k1_ref[...], k2_ref[...]], axis=0)
    vwin = jnp.concatenate([v0_ref[...], v1_ref[...], v2_ref[...]], axis=0)
    lane = lax.broadcasted_iota(jnp.int32, (1, ATTN_BAND), 1)
    n_chunks = TQ_ATTN // CHUNK

    def head(h):
        return slice(h * ATTN_HEAD_DIM, (h + 1) * ATTN_HEAD_DIM)

    def scores(c):
        key_pos = lane + (i * TQ_ATTN - ATTN_PREV + c * CHUNK)
        start_mask = jnp.where(key_pos >= 0, 0.0, NEG_INF)
        rows = slice(c * CHUNK, (c + 1) * CHUNK)
        band = slice(c * CHUNK, c * CHUNK + ATTN_BAND)
        return [lax.dot_general(q[rows, head(h)], kwin[band, head(h)],
                                (((1,), (1,)), ((), ())), preferred_element_type=F32)
                + bias_ref[h] + start_mask for h in range(ATTN_HEADS)]

    def finish(c, ss):
        band = slice(c * CHUNK, c * CHUNK + ATTN_BAND)
        ps = [jnp.exp(s - jnp.max(s, axis=-1, keepdims=True)) for s in ss]
        outs = [jnp.dot(p.astype(BF16), vwin[band, head(h)], preferred_element_type=F32)
                / jnp.sum(p, axis=-1, keepdims=True) for h, p in enumerate(ps)]
        o_ref[c * CHUNK:(c + 1) * CHUNK, :] = jnp.concatenate(outs, axis=1).astype(BF16)

    nxt = scores(0)
    for c in range(n_chunks):
        cur = nxt
        if c + 1 < n_chunks:
            nxt = scores(c + 1)
        finish(c, cur)


def _attn_branch(a, bias):
    nq = SEQ // TQ_ATTN
    prev_tiles = ATTN_PREV // TQ_ATTN

    def kv_spec(col, back):
        return pl.BlockSpec(
            (TQ_ATTN, ATTN_W),
            lambda b, i: (b * nq + jnp.maximum(i - back, 0), col))

    qcol, kcol, vcol = COL_Q // ATTN_W, COL_K // ATTN_W, COL_V // ATTN_W
    return pl.pallas_call(
        _attn_kernel,
        grid=(BATCH, nq),
        in_specs=[pl.BlockSpec((TQ_ATTN, ATTN_W), lambda b, i: (b * nq + i, qcol))]
        + [kv_spec(kcol, back) for back in range(prev_tiles, -1, -1)]
        + [kv_spec(vcol, back) for back in range(prev_tiles, -1, -1)]
        + [pl.BlockSpec((ATTN_HEADS, CHUNK, ATTN_BAND), lambda b, i: (0, 0, 0))],
        out_specs=pl.BlockSpec((TQ_ATTN, ATTN_W), lambda b, i: (b * nq + i, 0)),
        out_shape=jax.ShapeDtypeStruct((T, ATTN_W), BF16),
        compiler_params=_params("parallel", "parallel"),
        name="attn_branch",
    )(a, a, a, a, a, a, a, bias)


def _merge_kernel(with_router, *refs):
    if with_router:
        (h_ref, xc_ref, xs_ref, xa_ref, gc_ref, gs_ref, ga_ref, bg_ref, wc_ref, ws_ref,
         wa_ref, wo_ref, ng_ref, rhi_ref, rlo_ref, ho_ref, xn_ref, route_ref) = refs
    else:
        (h_ref, xc_ref, xs_ref, xa_ref, gc_ref, gs_ref, ga_ref, bg_ref, wc_ref, ws_ref,
         wa_ref, wo_ref, ng_ref, ho_ref, xn_ref) = refs

    merged = None
    for n, (x_ref, g_ref, w_ref) in enumerate(
            ((xc_ref, gc_ref, wc_ref), (xs_ref, gs_ref, ws_ref), (xa_ref, ga_ref, wa_ref))):
        y = jnp.dot(x_ref[...], w_ref[...], preferred_element_type=F32)
        gate = _sigmoid(g_ref[...].astype(F32) + bg_ref[:, n * D_MODEL:(n + 1) * D_MODEL])
        merged = gate * y if merged is None else merged + gate * y
    h = h_ref[...] + jnp.dot(merged.astype(BF16), wo_ref[...], preferred_element_type=F32)
    ho_ref[...] = h
    xn = _rms(h, ng_ref[...])
    xn_ref[...] = xn.astype(BF16)

    if with_router:
        x_hi = xn.astype(BF16)
        x_lo = (xn - x_hi.astype(F32)).astype(BF16)
        logits = (jnp.dot(x_hi, rhi_ref[...], preferred_element_type=F32)
                  + jnp.dot(x_lo, rhi_ref[...], preferred_element_type=F32)
                  + jnp.dot(x_hi, rlo_ref[...], preferred_element_type=F32))
        lane = lax.broadcasted_iota(jnp.int32, logits.shape, 1)
        logits = jnp.where(lane < N_EXPERTS, logits, NEG_INF)
        m1 = jnp.max(logits, axis=-1, keepdims=True)
        i1 = jnp.min(jnp.where(logits == m1, lane, ROUTE_LANES), axis=-1, keepdims=True)
        rest = jnp.where(lane == i1, NEG_INF, logits)
        m2 = jnp.max(rest, axis=-1, keepdims=True)
        i2 = jnp.min(jnp.where(rest == m2, lane, ROUTE_LANES), axis=-1, keepdims=True)
        e2 = jnp.exp(m2 - m1)
        w1 = 1.0 / (1.0 + e2)
        w2 = e2 / (1.0 + e2)
        route_ref[...] = jnp.where(
            lane == 0, i1.astype(F32),
            jnp.where(lane == 1, i2.astype(F32),
                      jnp.where(lane == 2, w1, jnp.where(lane == 3, w2, 0.0))))


def _merge(h, xc, xs, xa, a, b_gate, wc, ws, wa, wo, norm_g, router=None):
    with_router = router is not None
    row = lambda w: pl.BlockSpec((TM_MERGE, w), lambda i: (i, 0))
    gate = lambda n: pl.BlockSpec((TM_MERGE, D_MODEL), lambda i: (i, n))
    full = lambda r, c: pl.BlockSpec((r, c), lambda i: (0, 0))
    in_specs = [row(D_MODEL), row(CONV_W), row(SGU_W), row(ATTN_W),
                gate(0), gate(1), gate(2), full(1, N_GATE),
                full(CONV_W, D_MODEL), full(SGU_W, D_MODEL), full(ATTN_W, D_MODEL),
                full(D_MODEL, D_MODEL), full(1, D_MODEL)]
    args = [h, xc, xs, xa, a, a, a, b_gate, wc, ws, wa, wo, norm_g]
    out_specs = [row(D_MODEL), row(D_MODEL)]
    out_shape = [jax.ShapeDtypeStruct((T, D_MODEL), F32),
                 jax.ShapeDtypeStruct((T, D_MODEL), BF16)]
    if with_router:
        r = jnp.pad(router, ((0, 0), (0, ROUTE_LANES - N_EXPERTS)))
        r_hi = r.astype(BF16)
        r_lo = (r - r_hi.astype(F32)).astype(BF16)
        in_specs += [full(D_MODEL, ROUTE_LANES), full(D_MODEL, ROUTE_LANES)]
        args += [r_hi, r_lo]
        out_specs.append(row(ROUTE_LANES))
        out_shape.append(jax.ShapeDtypeStruct((T, ROUTE_LANES), F32))
    return pl.pallas_call(
        functools.partial(_merge_kernel, with_router),
        grid=(T // TM_MERGE,),
        in_specs=in_specs,
        out_specs=out_specs,
        out_shape=out_shape,
        compiler_params=_params("parallel"),
        name="merge_router" if with_router else "merge",
    )(*args)


def _swiglu_chunk(x, wg, wu, wd):
    g = jnp.dot(x, wg, preferred_element_type=F32)
    u = jnp.dot(x, wu, preferred_element_type=F32)
    act = (g * _sigmoid(g) * u).astype(BF16)
    return jnp.dot(act, wd, preferred_element_type=F32)


def _ffn_kernel(h_ref, x_ref, wg_ref, wu_ref, wd_ref, ng_ref, ho_ref, xn_ref):
    x = x_ref[...]
    acc = h_ref[...]
    for c in range(D_FF // TF_FFN):
        cols = slice(c * TF_FFN, (c + 1) * TF_FFN)
        acc = acc + _swiglu_chunk(x, wg_ref[:, cols], wu_ref[:, cols], wd_ref[cols, :])
    ho_ref[...] = acc
    xn_ref[...] = _rms(acc, ng_ref[...]).astype(BF16)


def _ffn(h, xn, wg, wu, wd, next_g):
    row = pl.BlockSpec((TM_FFN, D_MODEL), lambda i: (i, 0))
    return pl.pallas_call(
        _ffn_kernel,
        grid=(T // TM_FFN,),
        in_specs=[row, row,
                  _resident((D_MODEL, D_FF)), _resident((D_MODEL, D_FF)),
                  _resident((D_FF, D_MODEL)),
                  pl.BlockSpec((1, D_MODEL), lambda i: (0, 0))],
        out_specs=[row, row],
        out_shape=[jax.ShapeDtypeStruct((T, D_MODEL), F32),
                   jax.ShapeDtypeStruct((T, D_MODEL), BF16)],
        compiler_params=_params("parallel"),
        name="ffn",
    )(h, xn, wg, wu, wd, next_g)


def _moe_kernel(tile_expert_ref, n_tiles_ref, x_ref, wg_ref, wu_ref, wd_ref, o_ref, acc_ref):
    i, j = pl.program_id(0), pl.program_id(1)

    @pl.when(i < n_tiles_ref[0])
    def _():
        x = x_ref[...]
        y = None
        for c in range(TF_MOE // TC_MOE):
            cols = slice(c * TC_MOE, (c + 1) * TC_MOE)
            part = _swiglu_chunk(x, wg_ref[0, :, cols], wu_ref[0, :, cols], wd_ref[0, cols, :])
            y = part if y is None else y + part

        @pl.when(j == 0)
        def _():
            acc_ref[...] = y

        @pl.when(j > 0)
        def _():
            acc_ref[...] += y

        @pl.when(j == pl.num_programs(1) - 1)
        def _():
            o_ref[...] = acc_ref[...].astype(BF16)


def _moe(tile_expert, n_tiles, xs, wg, wu, wd):
    nj = D_FF_EXPERT // TF_MOE

    def jj(i, j, nt):
        return jnp.where(i < nt[0], j, nj - 1)

    grid_spec = pltpu.PrefetchScalarGridSpec(
        num_scalar_prefetch=2,
        grid=(NT_MOE, nj),
        in_specs=[
            pl.BlockSpec((TM_MOE, D_MODEL), lambda i, j, te, nt: (i, 0)),
            pl.BlockSpec((1, D_MODEL, TF_MOE), lambda i, j, te, nt: (te[i], 0, jj(i, j, nt))),
            pl.BlockSpec((1, D_MODEL, TF_MOE), lambda i, j, te, nt: (te[i], 0, jj(i, j, nt))),
            pl.BlockSpec((1, TF_MOE, D_MODEL), lambda i, j, te, nt: (te[i], jj(i, j, nt), 0)),
        ],
        out_specs=pl.BlockSpec((TM_MOE, D_MODEL), lambda i, j, te, nt: (i, 0)),
        scratch_shapes=[pltpu.VMEM((TM_MOE, D_MODEL), F32)],
    )
    return pl.pallas_call(
        _moe_kernel,
        grid_spec=grid_spec,
        out_shape=jax.ShapeDtypeStruct((P_MOE, D_MODEL), BF16),
        compiler_params=_params("arbitrary", "arbitrary"),
        name="moe_experts",
    )(tile_expert, n_tiles, xs, wg, wu, wd)


def _route(route):
    experts = jnp.concatenate([route[:, 0], route[:, 1]]).astype(jnp.int32)
    lanes = jnp.arange(N_EXPERTS, dtype=jnp.int32)[None, :]
    onehot = (experts[:, None] == lanes).astype(jnp.int32)
    running = jnp.cumsum(onehot, axis=0)
    rank = jnp.sum((running - 1) * onehot, axis=1)
    counts = running[-1]
    tiles = (counts + TM_MOE - 1) // TM_MOE
    tile_end = jnp.cumsum(tiles)
    row_start = (tile_end - tiles) * TM_MOE
    pos = jnp.sum(onehot * row_start[None, :], axis=1) + rank
    tile_expert = jnp.minimum(
        jnp.sum(jnp.arange(NT_MOE, dtype=jnp.int32)[:, None] >= tile_end[None, :], axis=1),
        N_EXPERTS - 1)
    pad = tiles * TM_MOE - counts
    pad_end = jnp.cumsum(pad)
    k = jnp.arange(P_MOE - TOP_K * T, dtype=jnp.int32)
    filler_expert = jnp.sum(k[:, None] >= pad_end[None, :], axis=1)
    filler_onehot = (filler_expert[:, None] == lanes).astype(jnp.int32)
    in_expert = jnp.sum(filler_onehot * (row_start + counts - (pad_end - pad))[None, :], axis=1) + k
    in_tail = tile_end[-1] * TM_MOE + (k - pad_end[-1])
    filler_pos = jnp.where(filler_expert < N_EXPERTS, in_expert, in_tail)
    token = jnp.arange(TOP_K * T, dtype=jnp.int32) % T
    _, row_token = lax.sort(
        (jnp.concatenate([pos, filler_pos]), jnp.concatenate([token, jnp.zeros_like(k)])),
        num_keys=1)
    return (tile_expert.astype(jnp.int32), tile_end[-1:].astype(jnp.int32), row_token,
            pos[:T], pos[T:])


def _combine_kernel(last, h_ref, y1_ref, y2_ref, route_ref, g_ref, *out_refs):
    w1 = route_ref[:, 2:3]
    w2 = route_ref[:, 3:4]
    h = h_ref[...] + w1 * y1_ref[...].astype(F32) + w2 * y2_ref[...].astype(F32)
    if last:
        out_refs[0][...] = _rms(h, g_ref[...])
    else:
        out_refs[0][...] = h
        out_refs[1][...] = _rms(h, g_ref[...]).astype(BF16)


def _combine(h, y1, y2, route, g, last):
    row = pl.BlockSpec((TM_NORM, D_MODEL), lambda i: (i, 0))
    out_specs = [row] if last else [row, row]
    out_shape = [jax.ShapeDtypeStruct((T, D_MODEL), F32)]
    if not last:
        out_shape.append(jax.ShapeDtypeStruct((T, D_MODEL), BF16))
    return pl.pallas_call(
        functools.partial(_combine_kernel, last),
        grid=(T // TM_NORM,),
        in_specs=[row, row, row,
                  pl.BlockSpec((TM_NORM, ROUTE_LANES), lambda i: (i, 0)),
                  pl.BlockSpec((1, D_MODEL), lambda i: (0, 0))],
        out_specs=out_specs,
        out_shape=out_shape,
        compiler_params=_params("parallel"),
        name="combine_final" if last else "combine",
    )(h, y1, y2, route, g)


def kernel(x, mix_norm_g, w_in, b_gate, conv_dw_w, conv_dw_b, conv_ln_g, conv_ln_b,
           conv_out_w, sgu_ln_g, sgu_ln_b, sgu_ws, sgu_bs, sgu_out_w, attn_rel_bias,
           attn_out_w, w_out, ffn_norm_g, ffn_w_gate, ffn_w_up, ffn_w_down, moe_router,
           moe_w_gate, moe_w_up, moe_w_down, final_norm_g):
    h = x.reshape(T, D_MODEL)
    xn = _norm(h, mix_norm_g[0][None], BF16)
    for layer in range(DEPTH):
        last = layer == DEPTH - 1
        next_g = (final_norm_g if last else mix_norm_g[layer + 1])[None]
        w_in_l = jnp.concatenate(
            [w_in[layer][:, N_MIX:], w_in[layer][:, :N_MIX]], axis=1).astype(BF16)
        a = _in_proj(xn, w_in_l)
        xc = _conv_branch(a, conv_dw_w[layer], conv_dw_b[layer][None],
                          conv_ln_g[layer][None], conv_ln_b[layer][None])
        sgu_bias = jnp.repeat(sgu_bs[layer].T, SGU_GW, axis=1)
        xs = _sgu_branch(a, sgu_ln_g[layer][None], sgu_ln_b[layer][None],
                         sgu_ws[layer], sgu_bias)
        xa = _attn_branch(a, _attn_bias(attn_rel_bias[layer]))

        i = layer // 2
        outs = _merge(h, xc, xs, xa, a, b_gate[layer][None],
                      conv_out_w[layer].astype(BF16), sgu_out_w[layer].astype(BF16),
                      attn_out_w[layer].astype(BF16), w_out[layer].astype(BF16),
                      ffn_norm_g[layer][None], moe_router[i] if layer % 2 == 1 else None)
        if layer % 2 == 0:
            h, xn2 = outs
            h, xn = _ffn(h, xn2, ffn_w_gate[i].astype(BF16), ffn_w_up[i].astype(BF16),
                         ffn_w_down[i].astype(BF16), next_g)
        else:
            h, xn2, route = outs
            tile_expert, n_tiles, row_token, pos1, pos2 = _route(route)
            ys = _moe(tile_expert, n_tiles, jnp.take(xn2, row_token, axis=0),
                      moe_w_gate[i].astype(BF16), moe_w_up[i].astype(BF16),
                      moe_w_down[i].astype(BF16))
            outs = _combine(h, jnp.take(ys, pos1, axis=0), jnp.take(ys, pos2, axis=0),
                            route, next_g, last)
            if last:
                h = outs[0]
            else:
                h, xn = outs
    return h.reshape(BATCH, SEQ, D_MODEL)
```

```python
import functools

import jax
import jax.numpy as jnp
from jax import lax
from jax.experimental import pallas as pl
from jax.experimental.pallas import tpu as pltpu

D_MODEL = 1024
BATCH = 4
SEQ = 8192
DEPTH = 4
CHUNK = 64
CONV_W = 512
CONV_K = 31
SGU_W = 512
SGU_GROUPS = 4
SGU_CHUNK = 128
SGU_GW = SGU_W // SGU_GROUPS
ATTN_HEADS = 8
ATTN_HEAD_DIM = 64
ATTN_W = ATTN_HEADS * ATTN_HEAD_DIM
ATTN_LEFT_CHUNKS = 8
REL_MIN = -(CHUNK - 1)
REL_MAX = 128
N_REL = REL_MAX - REL_MIN + 1
N_BRANCH = 3
N_GATE = N_BRANCH * D_MODEL
N_QKV = 3 * ATTN_W
N_MIX = 2 * CONV_W + 2 * SGU_W + N_QKV
N_IN = N_MIX + N_GATE
D_FF = 2816
N_EXPERTS = 8
TOP_K = 2
D_FF_EXPERT = 3584
EPS = 1e-6
NEG_INF = -1e30

T = BATCH * SEQ
F32 = jnp.float32
BF16 = jnp.bfloat16

SUBLANES = 8
LANES = 128

COL_CONV = N_GATE
COL_SGU = COL_CONV + 2 * CONV_W
COL_Q = COL_SGU + 2 * SGU_W

TM_NORM = 1024
TM_MIX = 512
TN_MIX = 512
CONV_HALO = 32
CONV_ROWS = 64
CONV_FIRST_TAP = CONV_HALO - (CONV_K - 1)
TQ_ATTN = 256
ATTN_PREV = ATTN_LEFT_CHUNKS * CHUNK
ATTN_BAND = (ATTN_LEFT_CHUNKS + 1) * CHUNK
ATTN_RVEC = 1024
TM_MERGE = 512
TM_FFN = 512
TF_FFN = 256
TM_MOE = 512
TF_MOE = 1792
TC_MOE = 256
NT_MOE = (TOP_K * T) // TM_MOE + N_EXPERTS
P_MOE = NT_MOE * TM_MOE
ROUTE_LANES = LANES
R_E1, R_E2, R_W1, R_W2, R_RANK1, R_RANK2 = range(6)

VMEM_LIMIT = 56 * 1024 * 1024


def _params(*sem):
    return pltpu.CompilerParams(dimension_semantics=sem, vmem_limit_bytes=VMEM_LIMIT)


def _resident(shape):
    return pl.BlockSpec(shape, lambda *_: (0,) * len(shape), pipeline_mode=pl.Buffered(1))


def _rms(x, g):
    return x * lax.rsqrt(jnp.mean(x * x, axis=-1, keepdims=True) + EPS) * g


def _layer_norm(x, g, b):
    mu = jnp.mean(x, axis=-1, keepdims=True)
    xc = x - mu
    var = jnp.mean(xc * xc, axis=-1, keepdims=True)
    return xc * lax.rsqrt(var + EPS) * g + b


def _sigmoid(x):
    return 1.0 / (1.0 + jnp.exp(-x))


def _norm_kernel(h_ref, g_ref, o_ref):
    o_ref[...] = _rms(h_ref[...], g_ref[...]).astype(o_ref.dtype)


def _norm(h, g, dtype):
    return pl.pallas_call(
        _norm_kernel,
        grid=(T // TM_NORM,),
        in_specs=[pl.BlockSpec((TM_NORM, D_MODEL), lambda i: (i, 0)),
                  pl.BlockSpec((1, D_MODEL), lambda i: (0, 0))],
        out_specs=pl.BlockSpec((TM_NORM, D_MODEL), lambda i: (i, 0)),
        out_shape=jax.ShapeDtypeStruct((T, D_MODEL), dtype),
        compiler_params=_params("parallel"),
        name="norm",
    )(h, g)


def _mix_in_kernel(x_ref, w_ref, cw_ref, cb_ref, clg_ref, clb_ref, slg_ref, slb_ref, sw_ref,
                   sbias_ref, gate_ref, qkv_ref, xc_ref, xs_ref, buf_ref, shift_ref):
    x = x_ref[...]

    def proj(c0, n):
        return jnp.dot(x, w_ref[:, c0:c0 + n], preferred_element_type=F32)

    ac = proj(COL_CONV, 2 * CONV_W)
    seq_start = (pl.program_id(0) % (SEQ // TM_MIX)) == 0
    buf_ref[0:CONV_HALO, :] = jnp.where(seq_start, 0.0, buf_ref[TM_MIX:TM_MIX + CONV_HALO, :])
    buf_ref[CONV_HALO:, :] = ac[:, :CONV_W] * _sigmoid(ac[:, CONV_W:])
    n_shift = shift_ref.shape[1]
    for s in range(1, SUBLANES):
        shift_ref[s] = buf_ref[pl.ds(s, n_shift), :]

    def conv_block(r0):
        acc = jnp.broadcast_to(cb_ref[...], (CONV_ROWS, CONV_W))
        for k in range(CONV_K):
            base, s = divmod(CONV_FIRST_TAP + k, SUBLANES)
            row = r0 + base * SUBLANES
            if s == 0:
                tap = buf_ref[pl.ds(row, CONV_ROWS), :]
            else:
                tap = shift_ref[s, pl.ds(row, CONV_ROWS), :]
            acc = acc + cw_ref[k:k + 1, :] * tap
        y = _layer_norm(acc, clg_ref[...], clb_ref[...])
        xc_ref[pl.ds(r0, CONV_ROWS), :] = (y * _sigmoid(y)).astype(BF16)

    def sgu():
        a = proj(COL_SGU, 2 * SGU_W)
        a = 0.5 * a * (1.0 + lax.erf(a * (0.5 ** 0.5)))
        u = a[:, :SGU_W]
        v = _layer_norm(a[:, SGU_W:], slg_ref[...], slb_ref[...]).astype(BF16)
        t_half = lax.broadcasted_iota(jnp.int32, (SGU_CHUNK, SGU_CHUNK), 0) // CHUNK
        s_half = lax.broadcasted_iota(jnp.int32, (SGU_CHUNK, SGU_CHUNK), 1) // CHUNK
        causal = s_half <= t_half
        n_chunks = TM_MIX // SGU_CHUNK
        mixed = []
        for g in range(SGU_GROUPS):
            w = jnp.where(causal, sw_ref[g], 0.0).astype(BF16)
            rhs = jnp.concatenate(
                [v[n * SGU_CHUNK:(n + 1) * SGU_CHUNK, g * SGU_GW:(g + 1) * SGU_GW]
                 for n in range(n_chunks)], axis=1)
            mixed.append(jnp.dot(w, rhs, preferred_element_type=F32))
        for n in range(n_chunks):
            m = jnp.concatenate(
                [mixed[g][:, n * SGU_GW:(n + 1) * SGU_GW] for g in range(SGU_GROUPS)], axis=1)
            rows = slice(n * SGU_CHUNK, (n + 1) * SGU_CHUNK)
            xs_ref[rows, :] = (u[rows, :] * (m + sbias_ref[...])).astype(BF16)

    blocks = list(range(0, TM_MIX, CONV_ROWS))
    chunks = ([(gate_ref, c, c) for c in range(0, N_GATE, TN_MIX)]
              + [(qkv_ref, c, COL_Q + c) for c in range(0, N_QKV, TN_MIX)])
    for n, (o_ref, oc, wc) in enumerate(chunks):
        o_ref[:, oc:oc + TN_MIX] = proj(wc, TN_MIX).astype(BF16)
        if n == 0:
            sgu()
        elif n - 1 < len(blocks):
            conv_block(blocks[n - 1])
    assert len(chunks) - 1 >= len(blocks)


def _mix_in(xn, w, cw, cb, clg, clb, slg, slb, sw, sbias):
    row = lambda n: pl.BlockSpec((TM_MIX, n), lambda i: (i, 0))
    vec = pl.BlockSpec((1, CONV_W), lambda i: (0, 0))
    n_buf = CONV_HALO + TM_MIX
    return pl.pallas_call(
        _mix_in_kernel,
        grid=(T // TM_MIX,),
        in_specs=[row(D_MODEL), _resident((D_MODEL, N_IN)),
                  pl.BlockSpec((CONV_K, CONV_W), lambda i: (0, 0)), vec, vec, vec, vec, vec,
                  pl.BlockSpec((SGU_GROUPS, SGU_CHUNK, SGU_CHUNK), lambda i: (0, 0, 0)),
                  pl.BlockSpec((SGU_CHUNK, SGU_W), lambda i: (0, 0))],
        out_specs=[row(N_GATE), row(N_QKV), row(CONV_W), row(SGU_W)],
        out_shape=[jax.ShapeDtypeStruct((T, N_GATE), BF16),
                   jax.ShapeDtypeStruct((T, N_QKV), BF16),
                   jax.ShapeDtypeStruct((T, CONV_W), BF16),
                   jax.ShapeDtypeStruct((T, SGU_W), BF16)],
        scratch_shapes=[pltpu.VMEM((n_buf, CONV_W), F32),
                        pltpu.VMEM((SUBLANES, n_buf - SUBLANES, CONV_W), F32)],
        compiler_params=_params("arbitrary"),
        name="mix_in",
    )(xn, w, cw, cb, clg, clb, slg, slb, sw, sbias)


def _attn_bias_kernel(rvec_ref, o_ref):
    x = jnp.broadcast_to(rvec_ref[0], (CHUNK, ATTN_RVEC))
    y = pltpu.roll(x, ATTN_RVEC - (CHUNK - 1), 1, stride=1, stride_axis=0)
    o_ref[0] = y[:, :ATTN_BAND]


def _attn_bias(rel_table):
    n_hi = ATTN_PREV + CHUNK - 1 - REL_MAX + 1
    n_lo = ATTN_RVEC - n_hi - (N_REL - 1)
    rvec = jnp.concatenate([
        jnp.broadcast_to(rel_table[:, N_REL - 1:], (ATTN_HEADS, n_hi)),
        rel_table[:, N_REL - 2:0:-1],
        jnp.broadcast_to(rel_table[:, :1], (ATTN_HEADS, n_lo + 1)),
    ], axis=1).astype(F32)
    return pl.pallas_call(
        _attn_bias_kernel,
        grid=(ATTN_HEADS,),
        in_specs=[pl.BlockSpec((1, 1, ATTN_RVEC), lambda h: (h, 0, 0))],
        out_specs=pl.BlockSpec((1, CHUNK, ATTN_BAND), lambda h: (h, 0, 0)),
        out_shape=jax.ShapeDtypeStruct((ATTN_HEADS, CHUNK, ATTN_BAND), F32),
        compiler_params=_params("parallel"),
        name="attn_bias",
    )(rvec[:, None, :])


def _attn_kernel(q_ref, k0_ref, k1_ref, k2_ref, v0_ref, v1_ref, v2_ref, bias_ref, o_ref):
    i = pl.program_id(1)
    q = q_ref[...] * (ATTN_HEAD_DIM ** -0.5)
    kwin = jnp.concatenate([k0_ref[...], k1_ref[...], k2_ref[...]], axis=0)
    vwin = jnp.concatenate([v0_ref[...], v1_ref[...], v2_ref[...]], axis=0)
    lane = lax.broadcasted_iota(jnp.int32, (1, ATTN_BAND), 1)
    n_chunks = TQ_ATTN // CHUNK

    def head(h):
        return slice(h * ATTN_HEAD_DIM, (h + 1) * ATTN_HEAD_DIM)

    def scores(c):
        key_pos = lane + (i * TQ_ATTN - ATTN_PREV + c * CHUNK)
        start_mask = jnp.where(key_pos >= 0, 0.0, NEG_INF)
        rows = slice(c * CHUNK, (c + 1) * CHUNK)
        band = slice(c * CHUNK, c * CHUNK + ATTN_BAND)
        return [lax.dot_general(q[rows, head(h)], kwin[band, head(h)],
                                (((1,), (1,)), ((), ())), preferred_element_type=F32)
                + bias_ref[h] + start_mask for h in range(ATTN_HEADS)]

    def finish(c, ss):
        band = slice(c * CHUNK, c * CHUNK + ATTN_BAND)
        ps = [jnp.exp(s - jnp.max(s, axis=-1, keepdims=True)) for s in ss]
        outs = [jnp.dot(p.astype(BF16), vwin[band, head(h)], preferred_element_type=F32)
                / jnp.sum(p, axis=-1, keepdims=True) for h, p in enumerate(ps)]
        o_ref[c * CHUNK:(c + 1) * CHUNK, :] = jnp.concatenate(outs, axis=1).astype(BF16)

    nxt = scores(0)
    for c in range(n_chunks):
        cur = nxt
        if c + 1 < n_chunks:
            nxt = scores(c + 1)
        finish(c, cur)


def _attn_branch(qkv, bias):
    nq = SEQ // TQ_ATTN
    prev_tiles = ATTN_PREV // TQ_ATTN

    def kv_spec(col, back):
        return pl.BlockSpec(
            (TQ_ATTN, ATTN_W),
            lambda b, i: (b * nq + jnp.maximum(i - back, 0), col))

    return pl.pallas_call(
        _attn_kernel,
        grid=(BATCH, nq),
        in_specs=[pl.BlockSpec((TQ_ATTN, ATTN_W), lambda b, i: (b * nq + i, 0))]
        + [kv_spec(1, back) for back in range(prev_tiles, -1, -1)]
        + [kv_spec(2, back) for back in range(prev_tiles, -1, -1)]
        + [pl.BlockSpec((ATTN_HEADS, CHUNK, ATTN_BAND), lambda b, i: (0, 0, 0))],
        out_specs=pl.BlockSpec((TQ_ATTN, ATTN_W), lambda b, i: (b * nq + i, 0)),
        out_shape=jax.ShapeDtypeStruct((T, ATTN_W), BF16),
        compiler_params=_params("parallel", "parallel"),
        name="attn_branch",
    )(qkv, qkv, qkv, qkv, qkv, qkv, qkv, bias)


def _merge_kernel(with_router, *refs):
    if with_router:
        (h_ref, xc_ref, xs_ref, xa_ref, gc_ref, gs_ref, ga_ref, bg_ref, wc_ref, ws_ref,
         wa_ref, wo_ref, ng_ref, rhi_ref, rlo_ref, ho_ref, xn_ref, route_ref, count_ref) = refs
    else:
        (h_ref, xc_ref, xs_ref, xa_ref, gc_ref, gs_ref, ga_ref, bg_ref, wc_ref, ws_ref,
         wa_ref, wo_ref, ng_ref, ho_ref, xn_ref) = refs

    merged = None
    for n, (x_ref, g_ref, w_ref) in enumerate(
            ((xc_ref, gc_ref, wc_ref), (xs_ref, gs_ref, ws_ref), (xa_ref, ga_ref, wa_ref))):
        y = jnp.dot(x_ref[...], w_ref[...], preferred_element_type=F32)
        gate = _sigmoid(g_ref[...].astype(F32) + bg_ref[:, n * D_MODEL:(n + 1) * D_MODEL])
        merged = gate * y if merged is None else merged + gate * y
    h = h_ref[...] + jnp.dot(merged.astype(BF16), wo_ref[...], preferred_element_type=F32)
    ho_ref[...] = h
    xn = _rms(h, ng_ref[...])
    xn_ref[...] = xn.astype(BF16)

    if with_router:
        x_hi = xn.astype(BF16)
        x_lo = (xn - x_hi.astype(F32)).astype(BF16)
        logits = (jnp.dot(x_hi, rhi_ref[...], preferred_element_type=F32)
                  + jnp.dot(x_lo, rhi_ref[...], preferred_element_type=F32)
                  + jnp.dot(x_hi, rlo_ref[...], preferred_element_type=F32))
        lane = lax.broadcasted_iota(jnp.int32, logits.shape, 1)
        logits = jnp.where(lane < N_EXPERTS, logits, NEG_INF)
        m1 = jnp.max(logits, axis=-1, keepdims=True)
        i1 = jnp.min(jnp.where(logits == m1, lane, ROUTE_LANES), axis=-1, keepdims=True)
        rest = jnp.where(lane == i1, NEG_INF, logits)
        m2 = jnp.max(rest, axis=-1, keepdims=True)
        i2 = jnp.min(jnp.where(rest == m2, lane, ROUTE_LANES), axis=-1, keepdims=True)
        e2 = jnp.exp(m2 - m1)
        w1 = 1.0 / (1.0 + e2)
        w2 = e2 / (1.0 + e2)

        @pl.when(pl.program_id(0) == 0)
        def _():
            count_ref[...] = jnp.zeros_like(count_ref)

        pick1, pick2 = lane == i1, lane == i2
        picked = jnp.where(pick1 | pick2, 1.0, 0.0)
        r_idx = lax.broadcasted_iota(jnp.int32, (TM_MERGE, TM_MERGE), 0)
        c_idx = lax.broadcasted_iota(jnp.int32, (TM_MERGE, TM_MERGE), 1)
        earlier = jnp.where(c_idx < r_idx, 1.0, 0.0).astype(BF16)
        before = count_ref[0:1, :] + jnp.dot(earlier, picked.astype(BF16),
                                             preferred_element_type=F32)
        rank1 = jnp.sum(jnp.where(pick1, before, 0.0), axis=-1, keepdims=True)
        rank2 = jnp.sum(jnp.where(pick2, before, 0.0), axis=-1, keepdims=True)
        count_ref[...] = jnp.broadcast_to(
            count_ref[0:1, :] + jnp.sum(picked, axis=0, keepdims=True), count_ref.shape)

        record = jnp.zeros_like(logits)
        for lane_id, value in ((R_E1, i1.astype(F32)), (R_E2, i2.astype(F32)), (R_W1, w1),
                               (R_W2, w2), (R_RANK1, rank1), (R_RANK2, rank2)):
            record = jnp.where(lane == lane_id, value, record)
        route_ref[...] = record


def _merge(h, xc, xs, xa, gates, b_gate, wc, ws, wa, wo, norm_g, router=None):
    with_router = router is not None
    row = lambda w: pl.BlockSpec((TM_MERGE, w), lambda i: (i, 0))
    gate = lambda n: pl.BlockSpec((TM_MERGE, D_MODEL), lambda i: (i, n))
    full = lambda r, c: pl.BlockSpec((r, c), lambda i: (0, 0))
    in_specs = [row(D_MODEL), row(CONV_W), row(SGU_W), row(ATTN_W),
                gate(0), gate(1), gate(2), full(1, N_GATE),
                full(CONV_W, D_MODEL), full(SGU_W, D_MODEL), full(ATTN_W, D_MODEL),
                full(D_MODEL, D_MODEL), full(1, D_MODEL)]
    args = [h, xc, xs, xa, gates, gates, gates, b_gate, wc, ws, wa, wo, norm_g]
    out_specs = [row(D_MODEL), row(D_MODEL)]
    out_shape = [jax.ShapeDtypeStruct((T, D_MODEL), F32),
                 jax.ShapeDtypeStruct((T, D_MODEL), BF16)]
    if with_router:
        r = jnp.pad(router, ((0, 0), (0, ROUTE_LANES - N_EXPERTS)))
        r_hi = r.astype(BF16)
        r_lo = (r - r_hi.astype(F32)).astype(BF16)
        in_specs += [full(D_MODEL, ROUTE_LANES), full(D_MODEL, ROUTE_LANES)]
        args += [r_hi, r_lo]
        out_specs += [row(ROUTE_LANES), full(SUBLANES, ROUTE_LANES)]
        out_shape += [jax.ShapeDtypeStruct((T, ROUTE_LANES), F32),
                      jax.ShapeDtypeStruct((SUBLANES, ROUTE_LANES), F32)]
    return pl.pallas_call(
        functools.partial(_merge_kernel, with_router),
        grid=(T // TM_MERGE,),
        in_specs=in_specs,
        out_specs=out_specs,
        out_shape=out_shape,
        compiler_params=_params("arbitrary" if with_router else "parallel"),
        name="merge_router" if with_router else "merge",
    )(*args)


def _swiglu_chunk(x, wg, wu, wd):
    g = jnp.dot(x, wg, preferred_element_type=F32)
    u = jnp.dot(x, wu, preferred_element_type=F32)
    act = (g * _sigmoid(g) * u).astype(BF16)
    return jnp.dot(act, wd, preferred_element_type=F32)


def _ffn_kernel(h_ref, x_ref, wg_ref, wu_ref, wd_ref, ng_ref, ho_ref, xn_ref):
    x = x_ref[...]
    acc = h_ref[...]
    for c in range(D_FF // TF_FFN):
        cols = slice(c * TF_FFN, (c + 1) * TF_FFN)
        acc = acc + _swiglu_chunk(x, wg_ref[:, cols], wu_ref[:, cols], wd_ref[cols, :])
    ho_ref[...] = acc
    xn_ref[...] = _rms(acc, ng_ref[...]).astype(BF16)


def _ffn(h, xn, wg, wu, wd, next_g):
    row = pl.BlockSpec((TM_FFN, D_MODEL), lambda i: (i, 0))
    return pl.pallas_call(
        _ffn_kernel,
        grid=(T // TM_FFN,),
        in_specs=[row, row,
                  _resident((D_MODEL, D_FF)), _resident((D_MODEL, D_FF)),
                  _resident((D_FF, D_MODEL)),
                  pl.BlockSpec((1, D_MODEL), lambda i: (0, 0))],
        out_specs=[row, row],
        out_shape=[jax.ShapeDtypeStruct((T, D_MODEL), F32),
                   jax.ShapeDtypeStruct((T, D_MODEL), BF16)],
        compiler_params=_params("parallel"),
        name="ffn",
    )(h, xn, wg, wu, wd, next_g)


def _moe_kernel(tile_expert_ref, n_tiles_ref, x_ref, wg_ref, wu_ref, wd_ref, o_ref, acc_ref):
    i, j = pl.program_id(0), pl.program_id(1)

    @pl.when(i < n_tiles_ref[0])
    def _():
        x = x_ref[...]
        y = None
        for c in range(TF_MOE // TC_MOE):
            cols = slice(c * TC_MOE, (c + 1) * TC_MOE)
            part = _swiglu_chunk(x, wg_ref[0, :, cols], wu_ref[0, :, cols], wd_ref[0, cols, :])
            y = part if y is None else y + part

        @pl.when(j == 0)
        def _():
            acc_ref[...] = y

        @pl.when(j > 0)
        def _():
            acc_ref[...] += y

        @pl.when(j == pl.num_programs(1) - 1)
        def _():
            o_ref[...] = acc_ref[...].astype(BF16)


def _moe(tile_expert, n_tiles, xs, wg, wu, wd):
    nj = D_FF_EXPERT // TF_MOE

    def jj(i, j, nt):
        return jnp.where(i < nt[0], j, nj - 1)

    grid_spec = pltpu.PrefetchScalarGridSpec(
        num_scalar_prefetch=2,
        grid=(NT_MOE, nj),
        in_specs=[
            pl.BlockSpec((TM_MOE, D_MODEL), lambda i, j, te, nt: (i, 0)),
            pl.BlockSpec((1, D_MODEL, TF_MOE), lambda i, j, te, nt: (te[i], 0, jj(i, j, nt))),
            pl.BlockSpec((1, D_MODEL, TF_MOE), lambda i, j, te, nt: (te[i], 0, jj(i, j, nt))),
            pl.BlockSpec((1, TF_MOE, D_MODEL), lambda i, j, te, nt: (te[i], jj(i, j, nt), 0)),
        ],
        out_specs=pl.BlockSpec((TM_MOE, D_MODEL), lambda i, j, te, nt: (i, 0)),
        scratch_shapes=[pltpu.VMEM((TM_MOE, D_MODEL), F32)],
    )
    return pl.pallas_call(
        _moe_kernel,
        grid_spec=grid_spec,
        out_shape=jax.ShapeDtypeStruct((P_MOE, D_MODEL), BF16),
        compiler_params=_params("arbitrary", "arbitrary"),
        name="moe_experts",
    )(tile_expert, n_tiles, xs, wg, wu, wd)


def _route(route, counts):
    counts = counts[0, :N_EXPERTS].astype(jnp.int32)
    tiles = (counts + TM_MOE - 1) // TM_MOE
    tile_end = jnp.cumsum(tiles)
    row_start = (tile_end - tiles) * TM_MOE

    def position(expert_lane, rank_lane):
        expert = route[:, expert_lane].astype(jnp.int32)
        start = sum(jnp.where(expert == e, row_start[e], 0) for e in range(N_EXPERTS))
        return start + route[:, rank_lane].astype(jnp.int32)

    pos1, pos2 = position(R_E1, R_RANK1), position(R_E2, R_RANK2)
    tile_expert = jnp.minimum(
        jnp.sum(jnp.arange(NT_MOE, dtype=jnp.int32)[:, None] >= tile_end[None, :], axis=1),
        N_EXPERTS - 1)
    pad = tiles * TM_MOE - counts
    pad_end = jnp.cumsum(pad)
    k = jnp.arange(P_MOE - TOP_K * T, dtype=jnp.int32)
    lanes = jnp.arange(N_EXPERTS, dtype=jnp.int32)[None, :]
    filler_expert = jnp.sum(k[:, None] >= pad_end[None, :], axis=1)
    filler_onehot = (filler_expert[:, None] == lanes).astype(jnp.int32)
    in_expert = jnp.sum(filler_onehot * (row_start + counts - (pad_end - pad))[None, :], axis=1) + k
    in_tail = tile_end[-1] * TM_MOE + (k - pad_end[-1])
    filler_pos = jnp.where(filler_expert < N_EXPERTS, in_expert, in_tail)
    token = jnp.arange(T, dtype=jnp.int32)
    _, row_token = lax.sort(
        (jnp.concatenate([pos1, pos2, filler_pos]),
         jnp.concatenate([token, token, jnp.zeros_like(k)])),
        num_keys=1)
    return (tile_expert.astype(jnp.int32), tile_end[-1:].astype(jnp.int32), row_token,
            pos1, pos2)


def _combine_kernel(last, h_ref, y1_ref, y2_ref, route_ref, g_ref, *out_refs):
    w1 = route_ref[:, R_W1:R_W1 + 1]
    w2 = route_ref[:, R_W2:R_W2 + 1]
    h = h_ref[...] + w1 * y1_ref[...].astype(F32) + w2 * y2_ref[...].astype(F32)
    if last:
        out_refs[0][...] = _rms(h, g_ref[...])
    else:
        out_refs[0][...] = h
        out_refs[1][...] = _rms(h, g_ref[...]).astype(BF16)


def _combine(h, y1, y2, route, g, last):
    row = pl.BlockSpec((TM_NORM, D_MODEL), lambda i: (i, 0))
    out_specs = [row] if last else [row, row]
    out_shape = [jax.ShapeDtypeStruct((T, D_MODEL), F32)]
    if not last:
        out_shape.append(jax.ShapeDtypeStruct((T, D_MODEL), BF16))
    return pl.pallas_call(
        functools.partial(_combine_kernel, last),
        grid=(T // TM_NORM,),
        in_specs=[row, row, row,
                  pl.BlockSpec((TM_NORM, ROUTE_LANES), lambda i: (i, 0)),
                  pl.BlockSpec((1, D_MODEL), lambda i: (0, 0))],
        out_specs=out_specs,
        out_shape=out_shape,
        compiler_params=_params("parallel"),
        name="combine_final" if last else "combine",
    )(h, y1, y2, route, g)


def kernel(x, mix_norm_g, w_in, b_gate, conv_dw_w, conv_dw_b, conv_ln_g, conv_ln_b,
           conv_out_w, sgu_ln_g, sgu_ln_b, sgu_ws, sgu_bs, sgu_out_w, attn_rel_bias,
           attn_out_w, w_out, ffn_norm_g, ffn_w_gate, ffn_w_up, ffn_w_down, moe_router,
           moe_w_gate, moe_w_up, moe_w_down, final_norm_g):
    h = x.reshape(T, D_MODEL)
    xn = _norm(h, mix_norm_g[0][None], BF16)
    for layer in range(DEPTH):
        last = layer == DEPTH - 1
        next_g = (final_norm_g if last else mix_norm_g[layer + 1])[None]
        w_in_l = jnp.concatenate(
            [w_in[layer][:, N_MIX:], w_in[layer][:, :N_MIX]], axis=1).astype(BF16)
        sgu_bias = jnp.repeat(sgu_bs[layer].T, SGU_GW, axis=1)
        gates, qkv, xc, xs = _mix_in(
            xn, w_in_l, conv_dw_w[layer], conv_dw_b[layer][None], conv_ln_g[layer][None],
            conv_ln_b[layer][None], sgu_ln_g[layer][None], sgu_ln_b[layer][None],
            sgu_ws[layer], sgu_bias)
        xa = _attn_branch(qkv, _attn_bias(attn_rel_bias[layer]))

        i = layer // 2
        outs = _merge(h, xc, xs, xa, gates, b_gate[layer][None],
                      conv_out_w[layer].astype(BF16), sgu_out_w[layer].astype(BF16),
                      attn_out_w[layer].astype(BF16), w_out[layer].astype(BF16),
                      ffn_norm_g[layer][None], moe_router[i] if layer % 2 == 1 else None)
        if layer % 2 == 0:
            h, xn2 = outs
            h, xn = _ffn(h, xn2, ffn_w_gate[i].astype(BF16), ffn_w_up[i].astype(BF16),
                         ffn_w_down[i].astype(BF16), next_g)
        else:
            h, xn2, route, counts = outs
            tile_expert, n_tiles, row_token, pos1, pos2 = _route(route, counts)
            ys = _moe(tile_expert, n_tiles, jnp.take(xn2, row_token, axis=0),
                      moe_w_gate[i].astype(BF16), moe_w_up[i].astype(BF16),
                      moe_w_down[i].astype(BF16))
            outs = _combine(h, jnp.take(ys, pos1, axis=0), jnp.take(ys, pos2, axis=0),
                            route, next_g, last)
            if last:
                h = outs[0]
            else:
                h, xn = outs
    return h.reshape(BATCH, SEQ, D_MODEL)
```

```python
import functools

import jax
import jax.numpy as jnp
from jax import lax
from jax.experimental import pallas as pl
from jax.experimental.pallas import tpu as pltpu

D_MODEL = 1024
BATCH = 4
SEQ = 8192
DEPTH = 4
CHUNK = 64
CONV_W = 512
CONV_K = 31
SGU_W = 512
SGU_GROUPS = 4
SGU_CHUNK = 128
SGU_GW = SGU_W // SGU_GROUPS
ATTN_HEADS = 8
ATTN_HEAD_DIM = 64
ATTN_W = ATTN_HEADS * ATTN_HEAD_DIM
ATTN_LEFT_CHUNKS = 8
REL_MIN = -(CHUNK - 1)
REL_MAX = 128
N_REL = REL_MAX - REL_MIN + 1
N_BRANCH = 3
N_GATE = N_BRANCH * D_MODEL
N_QKV = 3 * ATTN_W
N_MIX = 2 * CONV_W + 2 * SGU_W + N_QKV
N_IN = N_MIX + N_GATE
D_FF = 2816
N_EXPERTS = 8
TOP_K = 2
D_FF_EXPERT = 3584
EPS = 1e-6
NEG_INF = -1e30

T = BATCH * SEQ
F32 = jnp.float32
BF16 = jnp.bfloat16

SUBLANES = 8
LANES = 128

COL_CONV = N_GATE
COL_SGU = COL_CONV + 2 * CONV_W
COL_Q = COL_SGU + 2 * SGU_W

TM_NORM = 1024
TM_MIX = 512
TN_MIX = 512
CONV_HALO = 32
CONV_ROWS = 64
CONV_FIRST_TAP = CONV_HALO - (CONV_K - 1)
TQ_ATTN = 256
ATTN_PREV = ATTN_LEFT_CHUNKS * CHUNK
ATTN_BAND = (ATTN_LEFT_CHUNKS + 1) * CHUNK
ATTN_RVEC = 1024
TM_MERGE = 512
TM_FFN = 512
TF_FFN = 256
TM_MOE = 512
TC_MOE = 256
NT_MOE = (TOP_K * T) // TM_MOE + N_EXPERTS
P_MOE = NT_MOE * TM_MOE
MOE_PARTS = 2
assert NT_MOE % MOE_PARTS == 0
ROUTE_LANES = LANES
R_E1, R_E2, R_W1, R_W2, R_RANK1, R_RANK2 = range(6)

VMEM_LIMIT = 56 * 1024 * 1024


def _params(*sem):
    return pltpu.CompilerParams(dimension_semantics=sem, vmem_limit_bytes=VMEM_LIMIT)


def _resident(shape):
    return pl.BlockSpec(shape, lambda *_: (0,) * len(shape), pipeline_mode=pl.Buffered(1))


def _rms(x, g):
    return x * lax.rsqrt(jnp.mean(x * x, axis=-1, keepdims=True) + EPS) * g


def _layer_norm(x, g, b):
    mu = jnp.mean(x, axis=-1, keepdims=True)
    xc = x - mu
    var = jnp.mean(xc * xc, axis=-1, keepdims=True)
    return xc * lax.rsqrt(var + EPS) * g + b


def _sigmoid(x):
    return 1.0 / (1.0 + jnp.exp(-x))


def _norm_kernel(h_ref, g_ref, o_ref):
    o_ref[...] = _rms(h_ref[...], g_ref[...]).astype(o_ref.dtype)


def _norm(h, g, dtype):
    return pl.pallas_call(
        _norm_kernel,
        grid=(T // TM_NORM,),
        in_specs=[pl.BlockSpec((TM_NORM, D_MODEL), lambda i: (i, 0)),
                  pl.BlockSpec((1, D_MODEL), lambda i: (0, 0))],
        out_specs=pl.BlockSpec((TM_NORM, D_MODEL), lambda i: (i, 0)),
        out_shape=jax.ShapeDtypeStruct((T, D_MODEL), dtype),
        compiler_params=_params("parallel"),
        name="norm",
    )(h, g)


def _mix_in_kernel(x_ref, w_ref, cw_ref, cb_ref, clg_ref, clb_ref, slg_ref, slb_ref, sw_ref,
                   sbias_ref, gate_ref, qkv_ref, xc_ref, xs_ref, buf_ref, shift_ref):
    x = x_ref[...]

    def proj(c0, n):
        return jnp.dot(x, w_ref[:, c0:c0 + n], preferred_element_type=F32)

    ac = proj(COL_CONV, 2 * CONV_W)
    seq_start = (pl.program_id(0) % (SEQ // TM_MIX)) == 0
    buf_ref[0:CONV_HALO, :] = jnp.where(seq_start, 0.0, buf_ref[TM_MIX:TM_MIX + CONV_HALO, :])
    buf_ref[CONV_HALO:, :] = ac[:, :CONV_W] * _sigmoid(ac[:, CONV_W:])
    n_shift = shift_ref.shape[1]
    for s in range(1, SUBLANES):
        shift_ref[s] = buf_ref[pl.ds(s, n_shift), :]

    def conv_block(r0):
        acc = jnp.broadcast_to(cb_ref[...], (CONV_ROWS, CONV_W))
        for k in range(CONV_K):
            base, s = divmod(CONV_FIRST_TAP + k, SUBLANES)
            row = r0 + base * SUBLANES
            if s == 0:
                tap = buf_ref[pl.ds(row, CONV_ROWS), :]
            else:
                tap = shift_ref[s, pl.ds(row, CONV_ROWS), :]
            acc = acc + cw_ref[k:k + 1, :] * tap
        y = _layer_norm(acc, clg_ref[...], clb_ref[...])
        xc_ref[pl.ds(r0, CONV_ROWS), :] = (y * _sigmoid(y)).astype(BF16)

    def sgu():
        a = proj(COL_SGU, 2 * SGU_W)
        a = 0.5 * a * (1.0 + lax.erf(a * (0.5 ** 0.5)))
        u = a[:, :SGU_W]
        v = _layer_norm(a[:, SGU_W:], slg_ref[...], slb_ref[...]).astype(BF16)
        t_half = lax.broadcasted_iota(jnp.int32, (SGU_CHUNK, SGU_CHUNK), 0) // CHUNK
        s_half = lax.broadcasted_iota(jnp.int32, (SGU_CHUNK, SGU_CHUNK), 1) // CHUNK
        causal = s_half <= t_half
        n_chunks = TM_MIX // SGU_CHUNK
        mixed = []
        for g in range(SGU_GROUPS):
            w = jnp.where(causal, sw_ref[g], 0.0).astype(BF16)
            rhs = jnp.concatenate(
                [v[n * SGU_CHUNK:(n + 1) * SGU_CHUNK, g * SGU_GW:(g + 1) * SGU_GW]
                 for n in range(n_chunks)], axis=1)
            mixed.append(jnp.dot(w, rhs, preferred_element_type=F32))
        for n in range(n_chunks):
            m = jnp.concatenate(
                [mixed[g][:, n * SGU_GW:(n + 1) * SGU_GW] for g in range(SGU_GROUPS)], axis=1)
            rows = slice(n * SGU_CHUNK, (n + 1) * SGU_CHUNK)
            xs_ref[rows, :] = (u[rows, :] * (m + sbias_ref[...])).astype(BF16)

    blocks = list(range(0, TM_MIX, CONV_ROWS))
    chunks = ([(gate_ref, c, c) for c in range(0, N_GATE, TN_MIX)]
              + [(qkv_ref, c, COL_Q + c) for c in range(0, N_QKV, TN_MIX)])
    for n, (o_ref, oc, wc) in enumerate(chunks):
        o_ref[:, oc:oc + TN_MIX] = proj(wc, TN_MIX).astype(BF16)
        if n == 0:
            sgu()
        elif n - 1 < len(blocks):
            conv_block(blocks[n - 1])
    assert len(chunks) - 1 >= len(blocks)


def _mix_in(xn, w, cw, cb, clg, clb, slg, slb, sw, sbias):
    row = lambda n: pl.BlockSpec((TM_MIX, n), lambda i: (i, 0))
    vec = pl.BlockSpec((1, CONV_W), lambda i: (0, 0))
    n_buf = CONV_HALO + TM_MIX
    return pl.pallas_call(
        _mix_in_kernel,
        grid=(T // TM_MIX,),
        in_specs=[row(D_MODEL), _resident((D_MODEL, N_IN)),
                  pl.BlockSpec((CONV_K, CONV_W), lambda i: (0, 0)), vec, vec, vec, vec, vec,
                  pl.BlockSpec((SGU_GROUPS, SGU_CHUNK, SGU_CHUNK), lambda i: (0, 0, 0)),
                  pl.BlockSpec((SGU_CHUNK, SGU_W), lambda i: (0, 0))],
        out_specs=[row(N_GATE), row(N_QKV), row(CONV_W), row(SGU_W)],
        out_shape=[jax.ShapeDtypeStruct((T, N_GATE), BF16),
                   jax.ShapeDtypeStruct((T, N_QKV), BF16),
                   jax.ShapeDtypeStruct((T, CONV_W), BF16),
                   jax.ShapeDtypeStruct((T, SGU_W), BF16)],
        scratch_shapes=[pltpu.VMEM((n_buf, CONV_W), F32),
                        pltpu.VMEM((SUBLANES, n_buf - SUBLANES, CONV_W), F32)],
        compiler_params=_params("arbitrary"),
        name="mix_in",
    )(xn, w, cw, cb, clg, clb, slg, slb, sw, sbias)


def _attn_bias_kernel(rvec_ref, o_ref):
    x = jnp.broadcast_to(rvec_ref[0], (CHUNK, ATTN_RVEC))
    y = pltpu.roll(x, ATTN_RVEC - (CHUNK - 1), 1, stride=1, stride_axis=0)
    o_ref[0] = y[:, :ATTN_BAND]


def _attn_bias(rel_table):
    n_hi = ATTN_PREV + CHUNK - 1 - REL_MAX + 1
    n_lo = ATTN_RVEC - n_hi - (N_REL - 1)
    rvec = jnp.concatenate([
        jnp.broadcast_to(rel_table[:, N_REL - 1:], (ATTN_HEADS, n_hi)),
        rel_table[:, N_REL - 2:0:-1],
        jnp.broadcast_to(rel_table[:, :1], (ATTN_HEADS, n_lo + 1)),
    ], axis=1).astype(F32)
    return pl.pallas_call(
        _attn_bias_kernel,
        grid=(ATTN_HEADS,),
        in_specs=[pl.BlockSpec((1, 1, ATTN_RVEC), lambda h: (h, 0, 0))],
        out_specs=pl.BlockSpec((1, CHUNK, ATTN_BAND), lambda h: (h, 0, 0)),
        out_shape=jax.ShapeDtypeStruct((ATTN_HEADS, CHUNK, ATTN_BAND), F32),
        compiler_params=_params("parallel"),
        name="attn_bias",
    )(rvec[:, None, :])


def _attn_kernel(q_ref, k0_ref, k1_ref, k2_ref, v0_ref, v1_ref, v2_ref, bias_ref, o_ref):
    i = pl.program_id(1)
    q = q_ref[...] * (ATTN_HEAD_DIM ** -0.5)
    kwin = jnp.concatenate([k0_ref[...], k1_ref[...], k2_ref[...]], axis=0)
    vwin = jnp.concatenate([v0_ref[...], v1_ref[...], v2_ref[...]], axis=0)
    lane = lax.broadcasted_iota(jnp.int32, (1, ATTN_BAND), 1)
    n_chunks = TQ_ATTN // CHUNK

    def head(h):
        return slice(h * ATTN_HEAD_DIM, (h + 1) * ATTN_HEAD_DIM)

    def scores(c):
        key_pos = lane + (i * TQ_ATTN - ATTN_PREV + c * CHUNK)
        start_mask = jnp.where(key_pos >= 0, 0.0, NEG_INF)
        rows = slice(c * CHUNK, (c + 1) * CHUNK)
        band = slice(c * CHUNK, c * CHUNK + ATTN_BAND)
        return [lax.dot_general(q[rows, head(h)], kwin[band, head(h)],
                                (((1,), (1,)), ((), ())), preferred_element_type=F32)
                + bias_ref[h] + start_mask for h in range(ATTN_HEADS)]

    def finish(c, ss):
        band = slice(c * CHUNK, c * CHUNK + ATTN_BAND)
        ps = [jnp.exp(s - jnp.max(s, axis=-1, keepdims=True)) for s in ss]
        outs = [jnp.dot(p.astype(BF16), vwin[band, head(h)], preferred_element_type=F32)
                / jnp.sum(p, axis=-1, keepdims=True) for h, p in enumerate(ps)]
        o_ref[c * CHUNK:(c + 1) * CHUNK, :] = jnp.concatenate(outs, axis=1).astype(BF16)

    nxt = scores(0)
    for c in range(n_chunks):
        cur = nxt
        if c + 1 < n_chunks:
            nxt = scores(c + 1)
        finish(c, cur)


def _attn_branch(qkv, bias):
    nq = SEQ // TQ_ATTN
    prev_tiles = ATTN_PREV // TQ_ATTN

    def kv_spec(col, back):
        return pl.BlockSpec(
            (TQ_ATTN, ATTN_W),
            lambda b, i: (b * nq + jnp.maximum(i - back, 0), col))

    return pl.pallas_call(
        _attn_kernel,
        grid=(BATCH, nq),
        in_specs=[pl.BlockSpec((TQ_ATTN, ATTN_W), lambda b, i: (b * nq + i, 0))]
        + [kv_spec(1, back) for back in range(prev_tiles, -1, -1)]
        + [kv_spec(2, back) for back in range(prev_tiles, -1, -1)]
        + [pl.BlockSpec((ATTN_HEADS, CHUNK, ATTN_BAND), lambda b, i: (0, 0, 0))],
        out_specs=pl.BlockSpec((TQ_ATTN, ATTN_W), lambda b, i: (b * nq + i, 0)),
        out_shape=jax.ShapeDtypeStruct((T, ATTN_W), BF16),
        compiler_params=_params("parallel", "parallel"),
        name="attn_branch",
    )(qkv, qkv, qkv, qkv, qkv, qkv, qkv, bias)


def _merge_kernel(with_router, *refs):
    if with_router:
        (h_ref, xc_ref, xs_ref, xa_ref, gc_ref, gs_ref, ga_ref, bg_ref, wc_ref, ws_ref,
         wa_ref, wo_ref, ng_ref, rhi_ref, rlo_ref, ho_ref, xn_ref, route_ref, count_ref) = refs
    else:
        (h_ref, xc_ref, xs_ref, xa_ref, gc_ref, gs_ref, ga_ref, bg_ref, wc_ref, ws_ref,
         wa_ref, wo_ref, ng_ref, ho_ref, xn_ref) = refs

    merged = None
    for n, (x_ref, g_ref, w_ref) in enumerate(
            ((xc_ref, gc_ref, wc_ref), (xs_ref, gs_ref, ws_ref), (xa_ref, ga_ref, wa_ref))):
        y = jnp.dot(x_ref[...], w_ref[...], preferred_element_type=F32)
        gate = _sigmoid(g_ref[...].astype(F32) + bg_ref[:, n * D_MODEL:(n + 1) * D_MODEL])
        merged = gate * y if merged is None else merged + gate * y
    h = h_ref[...] + jnp.dot(merged.astype(BF16), wo_ref[...], preferred_element_type=F32)
    ho_ref[...] = h
    xn = _rms(h, ng_ref[...])
    xn_ref[...] = xn.astype(BF16)

    if with_router:
        x_hi = xn.astype(BF16)
        x_lo = (xn - x_hi.astype(F32)).astype(BF16)
        logits = (jnp.dot(x_hi, rhi_ref[...], preferred_element_type=F32)
                  + jnp.dot(x_lo, rhi_ref[...], preferred_element_type=F32)
                  + jnp.dot(x_hi, rlo_ref[...], preferred_element_type=F32))
        lane = lax.broadcasted_iota(jnp.int32, logits.shape, 1)
        logits = jnp.where(lane < N_EXPERTS, logits, NEG_INF)
        m1 = jnp.max(logits, axis=-1, keepdims=True)
        i1 = jnp.min(jnp.where(logits == m1, lane, ROUTE_LANES), axis=-1, keepdims=True)
        rest = jnp.where(lane == i1, NEG_INF, logits)
        m2 = jnp.max(rest, axis=-1, keepdims=True)
        i2 = jnp.min(jnp.where(rest == m2, lane, ROUTE_LANES), axis=-1, keepdims=True)
        e2 = jnp.exp(m2 - m1)
        w1 = 1.0 / (1.0 + e2)
        w2 = e2 / (1.0 + e2)

        @pl.when(pl.program_id(0) == 0)
        def _():
            count_ref[...] = jnp.zeros_like(count_ref)

        pick1, pick2 = lane == i1, lane == i2
        picked = jnp.where(pick1 | pick2, 1.0, 0.0)
        r_idx = lax.broadcasted_iota(jnp.int32, (TM_MERGE, TM_MERGE), 0)
        c_idx = lax.broadcasted_iota(jnp.int32, (TM_MERGE, TM_MERGE), 1)
        earlier = jnp.where(c_idx < r_idx, 1.0, 0.0).astype(BF16)
        before = count_ref[0:1, :] + jnp.dot(earlier, picked.astype(BF16),
                                             preferred_element_type=F32)
        rank1 = jnp.sum(jnp.where(pick1, before, 0.0), axis=-1, keepdims=True)
        rank2 = jnp.sum(jnp.where(pick2, before, 0.0), axis=-1, keepdims=True)
        count_ref[...] = jnp.broadcast_to(
            count_ref[0:1, :] + jnp.sum(picked, axis=0, keepdims=True), count_ref.shape)

        record = jnp.zeros_like(logits)
        for lane_id, value in ((R_E1, i1.astype(F32)), (R_E2, i2.astype(F32)), (R_W1, w1),
                               (R_W2, w2), (R_RANK1, rank1), (R_RANK2, rank2)):
            record = jnp.where(lane == lane_id, value, record)
        route_ref[...] = record


def _merge(h, xc, xs, xa, gates, b_gate, wc, ws, wa, wo, norm_g, router=None):
    with_router = router is not None
    row = lambda w: pl.BlockSpec((TM_MERGE, w), lambda i: (i, 0))
    gate = lambda n: pl.BlockSpec((TM_MERGE, D_MODEL), lambda i: (i, n))
    full = lambda r, c: pl.BlockSpec((r, c), lambda i: (0, 0))
    in_specs = [row(D_MODEL), row(CONV_W), row(SGU_W), row(ATTN_W),
                gate(0), gate(1), gate(2), full(1, N_GATE),
                full(CONV_W, D_MODEL), full(SGU_W, D_MODEL), full(ATTN_W, D_MODEL),
                full(D_MODEL, D_MODEL), full(1, D_MODEL)]
    args = [h, xc, xs, xa, gates, gates, gates, b_gate, wc, ws, wa, wo, norm_g]
    out_specs = [row(D_MODEL), row(D_MODEL)]
    out_shape = [jax.ShapeDtypeStruct((T, D_MODEL), F32),
                 jax.ShapeDtypeStruct((T, D_MODEL), BF16)]
    if with_router:
        r = jnp.pad(router, ((0, 0), (0, ROUTE_LANES - N_EXPERTS)))
        r_hi = r.astype(BF16)
        r_lo = (r - r_hi.astype(F32)).astype(BF16)
        in_specs += [full(D_MODEL, ROUTE_LANES), full(D_MODEL, ROUTE_LANES)]
        args += [r_hi, r_lo]
        out_specs += [row(ROUTE_LANES), full(SUBLANES, ROUTE_LANES)]
        out_shape += [jax.ShapeDtypeStruct((T, ROUTE_LANES), F32),
                      jax.ShapeDtypeStruct((SUBLANES, ROUTE_LANES), F32)]
    return pl.pallas_call(
        functools.partial(_merge_kernel, with_router),
        grid=(T // TM_MERGE,),
        in_specs=in_specs,
        out_specs=out_specs,
        out_shape=out_shape,
        compiler_params=_params("arbitrary" if with_router else "parallel"),
        name="merge_router" if with_router else "merge",
    )(*args)


def _swiglu_chunk(x, wg, wu, wd):
    g = jnp.dot(x, wg, preferred_element_type=F32)
    u = jnp.dot(x, wu, preferred_element_type=F32)
    act = (g * _sigmoid(g) * u).astype(BF16)
    return jnp.dot(act, wd, preferred_element_type=F32)


def _ffn_kernel(h_ref, x_ref, wg_ref, wu_ref, wd_ref, ng_ref, ho_ref, xn_ref):
    x = x_ref[...]
    acc = h_ref[...]
    for c in range(D_FF // TF_FFN):
        cols = slice(c * TF_FFN, (c + 1) * TF_FFN)
        acc = acc + _swiglu_chunk(x, wg_ref[:, cols], wu_ref[:, cols], wd_ref[cols, :])
    ho_ref[...] = acc
    xn_ref[...] = _rms(acc, ng_ref[...]).astype(BF16)


def _ffn(h, xn, wg, wu, wd, next_g):
    row = pl.BlockSpec((TM_FFN, D_MODEL), lambda i: (i, 0))
    return pl.pallas_call(
        _ffn_kernel,
        grid=(T // TM_FFN,),
        in_specs=[row, row,
                  _resident((D_MODEL, D_FF)), _resident((D_MODEL, D_FF)),
                  _resident((D_FF, D_MODEL)),
                  pl.BlockSpec((1, D_MODEL), lambda i: (0, 0))],
        out_specs=[row, row],
        out_shape=[jax.ShapeDtypeStruct((T, D_MODEL), F32),
                   jax.ShapeDtypeStruct((T, D_MODEL), BF16)],
        compiler_params=_params("parallel"),
        name="ffn",
    )(h, xn, wg, wu, wd, next_g)


def _moe_kernel(first_tile, tile_expert_ref, n_tiles_ref, x_ref, wg_ref, wu_ref, wd_ref,
                *rest):
    o_ref = rest[-1]

    @pl.when(first_tile + pl.program_id(0) < n_tiles_ref[0])
    def _():
        x = x_ref[...]
        y = None
        for c in range(D_FF_EXPERT // TC_MOE):
            cols = slice(c * TC_MOE, (c + 1) * TC_MOE)
            part = _swiglu_chunk(x, wg_ref[0, :, cols], wu_ref[0, :, cols], wd_ref[0, cols, :])
            y = part if y is None else y + part
        o_ref[...] = y.astype(BF16)


def _moe_part(tile_expert, n_tiles, xs, wg, wu, wd, ys, first_tile):
    w_spec = lambda r, c: pl.BlockSpec(
        (1, r, c), lambda i, te, nt: (te[first_tile + i], 0, 0))
    in_specs = [pl.BlockSpec((TM_MOE, D_MODEL), lambda i, te, nt: (i, 0)),
                w_spec(D_MODEL, D_FF_EXPERT), w_spec(D_MODEL, D_FF_EXPERT),
                w_spec(D_FF_EXPERT, D_MODEL)]
    args = [tile_expert, n_tiles, xs, wg, wu, wd]
    aliases = {}
    if ys is not None:
        in_specs.append(pl.BlockSpec(memory_space=pl.ANY))
        aliases = {len(args): 0}
        args.append(ys)
    grid_spec = pltpu.PrefetchScalarGridSpec(
        num_scalar_prefetch=2,
        grid=(xs.shape[0] // TM_MOE,),
        in_specs=in_specs,
        out_specs=pl.BlockSpec((TM_MOE, D_MODEL), lambda i, te, nt: (first_tile + i, 0)),
    )
    return pl.pallas_call(
        functools.partial(_moe_kernel, first_tile),
        grid_spec=grid_spec,
        out_shape=jax.ShapeDtypeStruct((P_MOE, D_MODEL), BF16),
        input_output_aliases=aliases,
        compiler_params=_params("arbitrary"),
        name="moe_experts",
    )(*args)


def _moe(tile_expert, n_tiles, xn, row_token, wg, wu, wd):
    part_rows = P_MOE // MOE_PARTS
    ys = None
    for p in range(MOE_PARTS):
        xs = _take_rows(xn, row_token[p * part_rows:(p + 1) * part_rows])
        ys = _moe_part(tile_expert, n_tiles, xs, wg, wu, wd, ys, p * (NT_MOE // MOE_PARTS))
    return ys


def _take_rows(x, rows):
    return x.at[rows].get(mode="promise_in_bounds")


def _route(route, counts):
    counts = counts[0, :N_EXPERTS].astype(jnp.int32)
    tiles = (counts + TM_MOE - 1) // TM_MOE
    tile_end = jnp.cumsum(tiles)
    row_start = (tile_end - tiles) * TM_MOE

    def position(expert_lane, rank_lane):
        expert = route[:, expert_lane].astype(jnp.int32)
        start = sum(jnp.where(expert == e, row_start[e], 0) for e in range(N_EXPERTS))
        return start + route[:, rank_lane].astype(jnp.int32)

    pos1, pos2 = position(R_E1, R_RANK1), position(R_E2, R_RANK2)
    tile_expert = jnp.minimum(
        jnp.sum(jnp.arange(NT_MOE, dtype=jnp.int32)[:, None] >= tile_end[None, :], axis=1),
        N_EXPERTS - 1)
    pad = tiles * TM_MOE - counts
    pad_end = jnp.cumsum(pad)
    k = jnp.arange(P_MOE - TOP_K * T, dtype=jnp.int32)
    lanes = jnp.arange(N_EXPERTS, dtype=jnp.int32)[None, :]
    filler_expert = jnp.sum(k[:, None] >= pad_end[None, :], axis=1)
    filler_onehot = (filler_expert[:, None] == lanes).astype(jnp.int32)
    in_expert = jnp.sum(filler_onehot * (row_start + counts - (pad_end - pad))[None, :], axis=1) + k
    in_tail = tile_end[-1] * TM_MOE + (k - pad_end[-1])
    filler_pos = jnp.where(filler_expert < N_EXPERTS, in_expert, in_tail)
    token = jnp.arange(T, dtype=jnp.int32)
    _, row_token = lax.sort(
        (jnp.concatenate([pos1, pos2, filler_pos]),
         jnp.concatenate([token, token, jnp.zeros_like(k)])),
        num_keys=1)
    return (tile_expert.astype(jnp.int32), tile_end[-1:].astype(jnp.int32), row_token,
            pos1, pos2)


def _combine_kernel(last, h_ref, y1_ref, y2_ref, route_ref, g_ref, *out_refs):
    w1 = route_ref[:, R_W1:R_W1 + 1]
    w2 = route_ref[:, R_W2:R_W2 + 1]
    h = h_ref[...] + w1 * y1_ref[...].astype(F32) + w2 * y2_ref[...].astype(F32)
    if last:
        out_refs[0][...] = _rms(h, g_ref[...])
    else:
        out_refs[0][...] = h
        out_refs[1][...] = _rms(h, g_ref[...]).astype(BF16)


def _combine(h, y1, y2, route, g, last):
    row = pl.BlockSpec((TM_NORM, D_MODEL), lambda i: (i, 0))
    out_specs = [row] if last else [row, row]
    out_shape = [jax.ShapeDtypeStruct((T, D_MODEL), F32)]
    if not last:
        out_shape.append(jax.ShapeDtypeStruct((T, D_MODEL), BF16))
    return pl.pallas_call(
        functools.partial(_combine_kernel, last),
        grid=(T // TM_NORM,),
        in_specs=[row, row, row,
                  pl.BlockSpec((TM_NORM, ROUTE_LANES), lambda i: (i, 0)),
                  pl.BlockSpec((1, D_MODEL), lambda i: (0, 0))],
        out_specs=out_specs,
        out_shape=out_shape,
        compiler_params=_params("parallel"),
        name="combine_final" if last else "combine",
    )(h, y1, y2, route, g)


def kernel(x, mix_norm_g, w_in, b_gate, conv_dw_w, conv_dw_b, conv_ln_g, conv_ln_b,
           conv_out_w, sgu_ln_g, sgu_ln_b, sgu_ws, sgu_bs, sgu_out_w, attn_rel_bias,
           attn_out_w, w_out, ffn_norm_g, ffn_w_gate, ffn_w_up, ffn_w_down, moe_router,
           moe_w_gate, moe_w_up, moe_w_down, final_norm_g):
    h = x.reshape(T, D_MODEL)
    xn = _norm(h, mix_norm_g[0][None], BF16)
    moe_wg = moe_w_gate.astype(BF16).reshape(-1, D_MODEL, D_FF_EXPERT)
    moe_wu = moe_w_up.astype(BF16).reshape(-1, D_MODEL, D_FF_EXPERT)
    moe_wd = moe_w_down.astype(BF16).reshape(-1, D_FF_EXPERT, D_MODEL)
    for layer in range(DEPTH):
        last = layer == DEPTH - 1
        next_g = (final_norm_g if last else mix_norm_g[layer + 1])[None]
        w_in_l = jnp.concatenate(
            [w_in[layer][:, N_MIX:], w_in[layer][:, :N_MIX]], axis=1).astype(BF16)
        sgu_bias = jnp.repeat(sgu_bs[layer].T, SGU_GW, axis=1)
        gates, qkv, xc, xs = _mix_in(
            xn, w_in_l, conv_dw_w[layer], conv_dw_b[layer][None], conv_ln_g[layer][None],
            conv_ln_b[layer][None], sgu_ln_g[layer][None], sgu_ln_b[layer][None],
            sgu_ws[layer], sgu_bias)
        xa = _attn_branch(qkv, _attn_bias(attn_rel_bias[layer]))

        i = layer // 2
        outs = _merge(h, xc, xs, xa, gates, b_gate[layer][None],
                      conv_out_w[layer].astype(BF16), sgu_out_w[layer].astype(BF16),
                      attn_out_w[layer].astype(BF16), w_out[layer].astype(BF16),
                      ffn_norm_g[layer][None], moe_router[i] if layer % 2 == 1 else None)
        if layer % 2 == 0:
            h, xn2 = outs
            h, xn = _ffn(h, xn2, ffn_w_gate[i].astype(BF16), ffn_w_up[i].astype(BF16),
                         ffn_w_down[i].astype(BF16), next_g)
        else:
            h, xn2, route, counts = outs
            tile_expert, n_tiles, row_token, pos1, pos2 = _route(route, counts)
            ys = _moe(tile_expert + i * N_EXPERTS, n_tiles, xn2, row_token,
                      moe_wg, moe_wu, moe_wd)
            outs = _combine(h, _take_rows(ys, pos1), _take_rows(ys, pos2),
                            route, next_g, last)
            if last:
                h = outs[0]
            else:
                h, xn = outs
    return h.reshape(BATCH, SEQ, D_MODEL)
```

```python
import functools

import jax
import jax.numpy as jnp
from jax import lax
from jax.experimental import pallas as pl
from jax.experimental.pallas import tpu as pltpu

D_MODEL = 1024
BATCH = 4
SEQ = 8192
DEPTH = 4
CHUNK = 64
CONV_W = 512
CONV_K = 31
SGU_W = 512
SGU_GROUPS = 4
SGU_CHUNK = 128
SGU_GW = SGU_W // SGU_GROUPS
ATTN_HEADS = 8
ATTN_HEAD_DIM = 64
ATTN_W = ATTN_HEADS * ATTN_HEAD_DIM
ATTN_LEFT_CHUNKS = 8
REL_MIN = -(CHUNK - 1)
REL_MAX = 128
N_REL = REL_MAX - REL_MIN + 1
N_BRANCH = 3
N_GATE = N_BRANCH * D_MODEL
N_QKV = 3 * ATTN_W
N_MIX = 2 * CONV_W + 2 * SGU_W + N_QKV
N_IN = N_MIX + N_GATE
D_FF = 2816
N_EXPERTS = 8
TOP_K = 2
D_FF_EXPERT = 3584
EPS = 1e-6
NEG_INF = -1e30

T = BATCH * SEQ
F32 = jnp.float32
BF16 = jnp.bfloat16

SUBLANES = 8
LANES = 128

COL_CONV = N_GATE
COL_SGU = COL_CONV + 2 * CONV_W
COL_Q = COL_SGU + 2 * SGU_W

TM_NORM = 1024
TM_MIX = 512
TN_MIX = 512
CONV_HALO = 32
CONV_ROWS = 64
CONV_FIRST_TAP = CONV_HALO - (CONV_K - 1)
TQ_ATTN = 256
ATTN_PREV = ATTN_LEFT_CHUNKS * CHUNK
ATTN_BAND = (ATTN_LEFT_CHUNKS + 1) * CHUNK
ATTN_RVEC = 1024
TM_MERGE = 512
TM_FFN = 512
TF_FFN = 256
TM_MOE = 512
TC_MOE = 256
NT_MOE = (TOP_K * T) // TM_MOE + N_EXPERTS
P_MOE = NT_MOE * TM_MOE
MOE_PARTS = 2
assert NT_MOE % MOE_PARTS == 0
ROUTE_LANES = LANES
R_E1, R_E2, R_W1, R_W2, R_RANK1, R_RANK2 = range(6)

VMEM_LIMIT = 56 * 1024 * 1024


def _params(*sem):
    return pltpu.CompilerParams(dimension_semantics=sem, vmem_limit_bytes=VMEM_LIMIT)


def _resident(shape):
    return pl.BlockSpec(shape, lambda *_: (0,) * len(shape), pipeline_mode=pl.Buffered(1))


def _rms(x, g):
    return x * lax.rsqrt(jnp.mean(x * x, axis=-1, keepdims=True) + EPS) * g


def _layer_norm(x, g, b):
    mu = jnp.mean(x, axis=-1, keepdims=True)
    xc = x - mu
    var = jnp.mean(xc * xc, axis=-1, keepdims=True)
    return xc * lax.rsqrt(var + EPS) * g + b


def _sigmoid(x):
    return 1.0 / (1.0 + jnp.exp(-x))


def _norm_kernel(h_ref, g_ref, o_ref):
    o_ref[...] = _rms(h_ref[...], g_ref[...]).astype(o_ref.dtype)


def _norm(h, g, dtype):
    return pl.pallas_call(
        _norm_kernel,
        grid=(T // TM_NORM,),
        in_specs=[pl.BlockSpec((TM_NORM, D_MODEL), lambda i: (i, 0)),
                  pl.BlockSpec((1, D_MODEL), lambda i: (0, 0))],
        out_specs=pl.BlockSpec((TM_NORM, D_MODEL), lambda i: (i, 0)),
        out_shape=jax.ShapeDtypeStruct((T, D_MODEL), dtype),
        compiler_params=_params("parallel"),
        name="norm",
    )(h, g)


def _mix_in_kernel(x_ref, w_ref, cw_ref, cb_ref, clg_ref, clb_ref, slg_ref, slb_ref, sw_ref,
                   sbias_ref, gate_ref, qkv_ref, xc_ref, xs_ref, buf_ref, shift_ref):
    x = x_ref[...]

    def proj(c0, n):
        return jnp.dot(x, w_ref[:, c0:c0 + n], preferred_element_type=F32)

    ac = proj(COL_CONV, 2 * CONV_W)
    seq_start = (pl.program_id(0) % (SEQ // TM_MIX)) == 0
    buf_ref[0:CONV_HALO, :] = jnp.where(seq_start, 0.0, buf_ref[TM_MIX:TM_MIX + CONV_HALO, :])
    buf_ref[CONV_HALO:, :] = ac[:, :CONV_W] * _sigmoid(ac[:, CONV_W:])
    n_shift = shift_ref.shape[1]
    for s in range(1, SUBLANES):
        shift_ref[s] = buf_ref[pl.ds(s, n_shift), :]

    def conv_block(r0):
        acc = jnp.broadcast_to(cb_ref[...], (CONV_ROWS, CONV_W))
        for k in range(CONV_K):
            base, s = divmod(CONV_FIRST_TAP + k, SUBLANES)
            row = r0 + base * SUBLANES
            if s == 0:
                tap = buf_ref[pl.ds(row, CONV_ROWS), :]
            else:
                tap = shift_ref[s, pl.ds(row, CONV_ROWS), :]
            acc = acc + cw_ref[k:k + 1, :] * tap
        y = _layer_norm(acc, clg_ref[...], clb_ref[...])
        xc_ref[pl.ds(r0, CONV_ROWS), :] = (y * _sigmoid(y)).astype(BF16)

    def sgu():
        a = proj(COL_SGU, 2 * SGU_W)
        a = 0.5 * a * (1.0 + lax.erf(a * (0.5 ** 0.5)))
        u = a[:, :SGU_W]
        v = _layer_norm(a[:, SGU_W:], slg_ref[...], slb_ref[...]).astype(BF16)
        t_half = lax.broadcasted_iota(jnp.int32, (SGU_CHUNK, SGU_CHUNK), 0) // CHUNK
        s_half = lax.broadcasted_iota(jnp.int32, (SGU_CHUNK, SGU_CHUNK), 1) // CHUNK
        causal = s_half <= t_half
        n_chunks = TM_MIX // SGU_CHUNK
        mixed = []
        for g in range(SGU_GROUPS):
            w = jnp.where(causal, sw_ref[g], 0.0).astype(BF16)
            rhs = jnp.concatenate(
                [v[n * SGU_CHUNK:(n + 1) * SGU_CHUNK, g * SGU_GW:(g + 1) * SGU_GW]
                 for n in range(n_chunks)], axis=1)
            mixed.append(jnp.dot(w, rhs, preferred_element_type=F32))
        for n in range(n_chunks):
            m = jnp.concatenate(
                [mixed[g][:, n * SGU_GW:(n + 1) * SGU_GW] for g in range(SGU_GROUPS)], axis=1)
            rows = slice(n * SGU_CHUNK, (n + 1) * SGU_CHUNK)
            xs_ref[rows, :] = (u[rows, :] * (m + sbias_ref[...])).astype(BF16)

    blocks = list(range(0, TM_MIX, CONV_ROWS))
    chunks = ([(gate_ref, c, c) for c in range(0, N_GATE, TN_MIX)]
              + [(qkv_ref, c, COL_Q + c) for c in range(0, N_QKV, TN_MIX)])
    first = len(chunks) - len(blocks) // 2
    for n, (o_ref, oc, wc) in enumerate(chunks):
        o_ref[:, oc:oc + TN_MIX] = proj(wc, TN_MIX).astype(BF16)
        if n == first - 1:
            sgu()
        elif n >= first:
            for r0 in blocks[(n - first) * 2:(n - first + 1) * 2]:
                conv_block(r0)


def _mix_in(xn, w, cw, cb, clg, clb, slg, slb, sw, sbias):
    row = lambda n: pl.BlockSpec((TM_MIX, n), lambda i: (i, 0))
    vec = pl.BlockSpec((1, CONV_W), lambda i: (0, 0))
    n_buf = CONV_HALO + TM_MIX
    return pl.pallas_call(
        _mix_in_kernel,
        grid=(T // TM_MIX,),
        in_specs=[row(D_MODEL), _resident((D_MODEL, N_IN)),
                  pl.BlockSpec((CONV_K, CONV_W), lambda i: (0, 0)), vec, vec, vec, vec, vec,
                  pl.BlockSpec((SGU_GROUPS, SGU_CHUNK, SGU_CHUNK), lambda i: (0, 0, 0)),
                  pl.BlockSpec((SGU_CHUNK, SGU_W), lambda i: (0, 0))],
        out_specs=[row(N_GATE), row(N_QKV), row(CONV_W), row(SGU_W)],
        out_shape=[jax.ShapeDtypeStruct((T, N_GATE), BF16),
                   jax.ShapeDtypeStruct((T, N_QKV), BF16),
                   jax.ShapeDtypeStruct((T, CONV_W), BF16),
                   jax.ShapeDtypeStruct((T, SGU_W), BF16)],
        scratch_shapes=[pltpu.VMEM((n_buf, CONV_W), F32),
                        pltpu.VMEM((SUBLANES, n_buf - SUBLANES, CONV_W), F32)],
        compiler_params=_params("arbitrary"),
        name="mix_in",
    )(xn, w, cw, cb, clg, clb, slg, slb, sw, sbias)


def _attn_bias_kernel(rvec_ref, o_ref):
    x = jnp.broadcast_to(rvec_ref[0], (CHUNK, ATTN_RVEC))
    y = pltpu.roll(x, ATTN_RVEC - (CHUNK - 1), 1, stride=1, stride_axis=0)
    o_ref[0] = y[:, :ATTN_BAND]


def _attn_bias(rel_table):
    n_hi = ATTN_PREV + CHUNK - 1 - REL_MAX + 1
    n_lo = ATTN_RVEC - n_hi - (N_REL - 1)
    rvec = jnp.concatenate([
        jnp.broadcast_to(rel_table[:, N_REL - 1:], (ATTN_HEADS, n_hi)),
        rel_table[:, N_REL - 2:0:-1],
        jnp.broadcast_to(rel_table[:, :1], (ATTN_HEADS, n_lo + 1)),
    ], axis=1).astype(F32)
    return pl.pallas_call(
        _attn_bias_kernel,
        grid=(ATTN_HEADS,),
        in_specs=[pl.BlockSpec((1, 1, ATTN_RVEC), lambda h: (h, 0, 0))],
        out_specs=pl.BlockSpec((1, CHUNK, ATTN_BAND), lambda h: (h, 0, 0)),
        out_shape=jax.ShapeDtypeStruct((ATTN_HEADS, CHUNK, ATTN_BAND), F32),
        compiler_params=_params("parallel"),
        name="attn_bias",
    )(rvec[:, None, :])


def _attn_kernel(q_ref, k0_ref, k1_ref, k2_ref, v0_ref, v1_ref, v2_ref, bias_ref, o_ref):
    i = pl.program_id(1)
    q = q_ref[...] * (ATTN_HEAD_DIM ** -0.5)
    kwin = jnp.concatenate([k0_ref[...], k1_ref[...], k2_ref[...]], axis=0)
    vwin = jnp.concatenate([v0_ref[...], v1_ref[...], v2_ref[...]], axis=0)
    lane = lax.broadcasted_iota(jnp.int32, (1, ATTN_BAND), 1)
    n_chunks = TQ_ATTN // CHUNK

    def head(h):
        return slice(h * ATTN_HEAD_DIM, (h + 1) * ATTN_HEAD_DIM)

    def scores(c):
        key_pos = lane + (i * TQ_ATTN - ATTN_PREV + c * CHUNK)
        start_mask = jnp.where(key_pos >= 0, 0.0, NEG_INF)
        rows = slice(c * CHUNK, (c + 1) * CHUNK)
        band = slice(c * CHUNK, c * CHUNK + ATTN_BAND)
        return [lax.dot_general(q[rows, head(h)], kwin[band, head(h)],
                                (((1,), (1,)), ((), ())), preferred_element_type=F32)
                + bias_ref[h] + start_mask for h in range(ATTN_HEADS)]

    def finish(c, ss):
        band = slice(c * CHUNK, c * CHUNK + ATTN_BAND)
        ps = [jnp.exp(s - jnp.max(s, axis=-1, keepdims=True)) for s in ss]
        outs = [jnp.dot(p.astype(BF16), vwin[band, head(h)], preferred_element_type=F32)
                / jnp.sum(p, axis=-1, keepdims=True) for h, p in enumerate(ps)]
        o_ref[c * CHUNK:(c + 1) * CHUNK, :] = jnp.concatenate(outs, axis=1).astype(BF16)

    nxt = scores(0)
    for c in range(n_chunks):
        cur = nxt
        if c + 1 < n_chunks:
            nxt = scores(c + 1)
        finish(c, cur)


def _attn_branch(qkv, bias):
    nq = SEQ // TQ_ATTN
    prev_tiles = ATTN_PREV // TQ_ATTN

    def kv_spec(col, back):
        return pl.BlockSpec(
            (TQ_ATTN, ATTN_W),
            lambda b, i: (b * nq + jnp.maximum(i - back, 0), col))

    return pl.pallas_call(
        _attn_kernel,
        grid=(BATCH, nq),
        in_specs=[pl.BlockSpec((TQ_ATTN, ATTN_W), lambda b, i: (b * nq + i, 0))]
        + [kv_spec(1, back) for back in range(prev_tiles, -1, -1)]
        + [kv_spec(2, back) for back in range(prev_tiles, -1, -1)]
        + [pl.BlockSpec((ATTN_HEADS, CHUNK, ATTN_BAND), lambda b, i: (0, 0, 0))],
        out_specs=pl.BlockSpec((TQ_ATTN, ATTN_W), lambda b, i: (b * nq + i, 0)),
        out_shape=jax.ShapeDtypeStruct((T, ATTN_W), BF16),
        compiler_params=_params("parallel", "parallel"),
        name="attn_branch",
    )(qkv, qkv, qkv, qkv, qkv, qkv, qkv, bias)


def _merge_kernel(with_router, *refs):
    if with_router:
        (h_ref, xc_ref, xs_ref, xa_ref, gc_ref, gs_ref, ga_ref, bg_ref, wc_ref, ws_ref,
         wa_ref, wo_ref, ng_ref, rcat_ref, ho_ref, xn_ref, route_ref, count_ref) = refs
    else:
        (h_ref, xc_ref, xs_ref, xa_ref, gc_ref, gs_ref, ga_ref, bg_ref, wc_ref, ws_ref,
         wa_ref, wo_ref, ng_ref, ho_ref, xn_ref) = refs

    merged = None
    for n, (x_ref, g_ref, w_ref) in enumerate(
            ((xc_ref, gc_ref, wc_ref), (xs_ref, gs_ref, ws_ref), (xa_ref, ga_ref, wa_ref))):
        y = jnp.dot(x_ref[...], w_ref[...], preferred_element_type=F32)
        gate = _sigmoid(g_ref[...].astype(F32) + bg_ref[:, n * D_MODEL:(n + 1) * D_MODEL])
        merged = gate * y if merged is None else merged + gate * y
    h = h_ref[...] + jnp.dot(merged.astype(BF16), wo_ref[...], preferred_element_type=F32)
    ho_ref[...] = h
    xn = _rms(h, ng_ref[...])
    xn_ref[...] = xn.astype(BF16)

    if with_router:
        x_hi = xn.astype(BF16)
        x_lo = (xn - x_hi.astype(F32)).astype(BF16)
        both = jnp.dot(x_hi, rcat_ref[...], preferred_element_type=F32)
        logits = (both[:, :ROUTE_LANES] + both[:, ROUTE_LANES:]
                  + jnp.dot(x_lo, rcat_ref[:, :ROUTE_LANES], preferred_element_type=F32))
        lane = lax.broadcasted_iota(jnp.int32, logits.shape, 1)
        logits = jnp.where(lane < N_EXPERTS, logits, NEG_INF)
        m1 = jnp.max(logits, axis=-1, keepdims=True)
        i1 = jnp.min(jnp.where(logits == m1, lane, ROUTE_LANES), axis=-1, keepdims=True)
        rest = jnp.where(lane == i1, NEG_INF, logits)
        m2 = jnp.max(rest, axis=-1, keepdims=True)
        i2 = jnp.min(jnp.where(rest == m2, lane, ROUTE_LANES), axis=-1, keepdims=True)
        e2 = jnp.exp(m2 - m1)
        w1 = 1.0 / (1.0 + e2)
        w2 = e2 / (1.0 + e2)

        @pl.when(pl.program_id(0) == 0)
        def _():
            count_ref[...] = jnp.zeros_like(count_ref)

        pick1, pick2 = lane == i1, lane == i2
        picked = jnp.where(pick1 | pick2, 1.0, 0.0)
        r_idx = lax.broadcasted_iota(jnp.int32, (TM_MERGE, TM_MERGE), 0)
        c_idx = lax.broadcasted_iota(jnp.int32, (TM_MERGE, TM_MERGE), 1)
        earlier = jnp.where(c_idx < r_idx, 1.0, 0.0).astype(BF16)
        before = count_ref[0:1, :] + jnp.dot(earlier, picked.astype(BF16),
                                             preferred_element_type=F32)
        rank1 = jnp.sum(jnp.where(pick1, before, 0.0), axis=-1, keepdims=True)
        rank2 = jnp.sum(jnp.where(pick2, before, 0.0), axis=-1, keepdims=True)
        count_ref[...] = jnp.broadcast_to(
            count_ref[0:1, :] + jnp.sum(picked, axis=0, keepdims=True), count_ref.shape)

        record = jnp.zeros_like(logits)
        for lane_id, value in ((R_E1, i1.astype(F32)), (R_E2, i2.astype(F32)), (R_W1, w1),
                               (R_W2, w2), (R_RANK1, rank1), (R_RANK2, rank2)):
            record = jnp.where(lane == lane_id, value, record)
        route_ref[...] = record


def _merge(h, xc, xs, xa, gates, b_gate, wc, ws, wa, wo, norm_g, router=None):
    with_router = router is not None
    row = lambda w: pl.BlockSpec((TM_MERGE, w), lambda i: (i, 0))
    gate = lambda n: pl.BlockSpec((TM_MERGE, D_MODEL), lambda i: (i, n))
    full = lambda r, c: pl.BlockSpec((r, c), lambda i: (0, 0))
    in_specs = [row(D_MODEL), row(CONV_W), row(SGU_W), row(ATTN_W),
                gate(0), gate(1), gate(2), full(1, N_GATE),
                full(CONV_W, D_MODEL), full(SGU_W, D_MODEL), full(ATTN_W, D_MODEL),
                full(D_MODEL, D_MODEL), full(1, D_MODEL)]
    args = [h, xc, xs, xa, gates, gates, gates, b_gate, wc, ws, wa, wo, norm_g]
    out_specs = [row(D_MODEL), row(D_MODEL)]
    out_shape = [jax.ShapeDtypeStruct((T, D_MODEL), F32),
                 jax.ShapeDtypeStruct((T, D_MODEL), BF16)]
    if with_router:
        r = jnp.pad(router, ((0, 0), (0, ROUTE_LANES - N_EXPERTS)))
        r_hi = r.astype(BF16)
        r_lo = (r - r_hi.astype(F32)).astype(BF16)
        in_specs.append(full(D_MODEL, 2 * ROUTE_LANES))
        args.append(jnp.concatenate([r_hi, r_lo], axis=1))
        out_specs += [row(ROUTE_LANES), full(SUBLANES, ROUTE_LANES)]
        out_shape += [jax.ShapeDtypeStruct((T, ROUTE_LANES), F32),
                      jax.ShapeDtypeStruct((SUBLANES, ROUTE_LANES), F32)]
    return pl.pallas_call(
        functools.partial(_merge_kernel, with_router),
        grid=(T // TM_MERGE,),
        in_specs=in_specs,
        out_specs=out_specs,
        out_shape=out_shape,
        compiler_params=_params("arbitrary" if with_router else "parallel"),
        name="merge_router" if with_router else "merge",
    )(*args)


def _swiglu_chunk(x, wg, wu, wd):
    g = jnp.dot(x, wg, preferred_element_type=F32)
    u = jnp.dot(x, wu, preferred_element_type=F32)
    act = (g * _sigmoid(g) * u).astype(BF16)
    return jnp.dot(act, wd, preferred_element_type=F32)


def _ffn_kernel(h_ref, x_ref, wg_ref, wu_ref, wd_ref, ng_ref, ho_ref, xn_ref):
    x = x_ref[...]
    acc = h_ref[...]
    for c in range(D_FF // TF_FFN):
        cols = slice(c * TF_FFN, (c + 1) * TF_FFN)
        acc = acc + _swiglu_chunk(x, wg_ref[:, cols], wu_ref[:, cols], wd_ref[cols, :])
    ho_ref[...] = acc
    xn_ref[...] = _rms(acc, ng_ref[...]).astype(BF16)


def _ffn(h, xn, wg, wu, wd, next_g):
    row = pl.BlockSpec((TM_FFN, D_MODEL), lambda i: (i, 0))
    return pl.pallas_call(
        _ffn_kernel,
        grid=(T // TM_FFN,),
        in_specs=[row, row,
                  _resident((D_MODEL, D_FF)), _resident((D_MODEL, D_FF)),
                  _resident((D_FF, D_MODEL)),
                  pl.BlockSpec((1, D_MODEL), lambda i: (0, 0))],
        out_specs=[row, row],
        out_shape=[jax.ShapeDtypeStruct((T, D_MODEL), F32),
                   jax.ShapeDtypeStruct((T, D_MODEL), BF16)],
        compiler_params=_params("parallel"),
        name="ffn",
    )(h, xn, wg, wu, wd, next_g)


def _moe_kernel(first_tile, tile_expert_ref, n_tiles_ref, x_ref, wg_ref, wu_ref, wd_ref,
                *rest):
    o_ref = rest[-1]

    @pl.when(first_tile + pl.program_id(0) < n_tiles_ref[0])
    def _():
        x = x_ref[...]
        y = None
        for c in range(D_FF_EXPERT // TC_MOE):
            cols = slice(c * TC_MOE, (c + 1) * TC_MOE)
            part = _swiglu_chunk(x, wg_ref[0, :, cols], wu_ref[0, :, cols], wd_ref[0, cols, :])
            y = part if y is None else y + part
        o_ref[...] = y.astype(BF16)


def _moe_part(tile_expert, n_tiles, xs, wg, wu, wd, ys, first_tile):
    w_spec = lambda r, c: pl.BlockSpec(
        (1, r, c), lambda i, te, nt: (te[first_tile + i], 0, 0))
    in_specs = [pl.BlockSpec((TM_MOE, D_MODEL), lambda i, te, nt: (i, 0)),
                w_spec(D_MODEL, D_FF_EXPERT), w_spec(D_MODEL, D_FF_EXPERT),
                w_spec(D_FF_EXPERT, D_MODEL)]
    args = [tile_expert, n_tiles, xs, wg, wu, wd]
    aliases = {}
    if ys is not None:
        in_specs.append(pl.BlockSpec(memory_space=pl.ANY))
        aliases = {len(args): 0}
        args.append(ys)
    grid_spec = pltpu.PrefetchScalarGridSpec(
        num_scalar_prefetch=2,
        grid=(xs.shape[0] // TM_MOE,),
        in_specs=in_specs,
        out_specs=pl.BlockSpec((TM_MOE, D_MODEL), lambda i, te, nt: (first_tile + i, 0)),
    )
    return pl.pallas_call(
        functools.partial(_moe_kernel, first_tile),
        grid_spec=grid_spec,
        out_shape=jax.ShapeDtypeStruct((P_MOE, D_MODEL), BF16),
        input_output_aliases=aliases,
        compiler_params=_params("arbitrary"),
        name="moe_experts",
    )(*args)


def _moe(tile_expert, n_tiles, xn, row_token, wg, wu, wd):
    part_rows = P_MOE // MOE_PARTS
    ys = None
    for p in range(MOE_PARTS):
        xs = _take_rows(xn, row_token[p * part_rows:(p + 1) * part_rows])
        ys = _moe_part(tile_expert, n_tiles, xs, wg, wu, wd, ys, p * (NT_MOE // MOE_PARTS))
    return ys


def _take_rows(x, rows):
    return x.at[rows].get(mode="promise_in_bounds")


def _route(route, counts):
    counts = counts[0, :N_EXPERTS].astype(jnp.int32)
    tiles = (counts + TM_MOE - 1) // TM_MOE
    tile_end = jnp.cumsum(tiles)
    row_start = (tile_end - tiles) * TM_MOE

    def position(expert_lane, rank_lane):
        expert = route[:, expert_lane].astype(jnp.int32)
        start = sum(jnp.where(expert == e, row_start[e], 0) for e in range(N_EXPERTS))
        return start + route[:, rank_lane].astype(jnp.int32)

    pos1, pos2 = position(R_E1, R_RANK1), position(R_E2, R_RANK2)
    tile_expert = jnp.minimum(
        jnp.sum(jnp.arange(NT_MOE, dtype=jnp.int32)[:, None] >= tile_end[None, :], axis=1),
        N_EXPERTS - 1)
    pad = tiles * TM_MOE - counts
    pad_end = jnp.cumsum(pad)
    k = jnp.arange(P_MOE - TOP_K * T, dtype=jnp.int32)
    lanes = jnp.arange(N_EXPERTS, dtype=jnp.int32)[None, :]
    filler_expert = jnp.sum(k[:, None] >= pad_end[None, :], axis=1)
    filler_onehot = (filler_expert[:, None] == lanes).astype(jnp.int32)
    in_expert = jnp.sum(filler_onehot * (row_start + counts - (pad_end - pad))[None, :], axis=1) + k
    in_tail = tile_end[-1] * TM_MOE + (k - pad_end[-1])
    filler_pos = jnp.where(filler_expert < N_EXPERTS, in_expert, in_tail)
    token = jnp.arange(T, dtype=jnp.int32)
    _, row_token = lax.sort(
        (jnp.concatenate([pos1, pos2, filler_pos]),
         jnp.concatenate([token, token, jnp.zeros_like(k)])),
        num_keys=1)
    return (tile_expert.astype(jnp.int32), tile_end[-1:].astype(jnp.int32), row_token,
            pos1, pos2)


def _combine_kernel(last, h_ref, y1_ref, y2_ref, route_ref, g_ref, *out_refs):
    w1 = route_ref[:, R_W1:R_W1 + 1]
    w2 = route_ref[:, R_W2:R_W2 + 1]
    h = h_ref[...] + w1 * y1_ref[...].astype(F32) + w2 * y2_ref[...].astype(F32)
    if last:
        out_refs[0][...] = _rms(h, g_ref[...])
    else:
        out_refs[0][...] = h
        out_refs[1][...] = _rms(h, g_ref[...]).astype(BF16)


def _combine(h, y1, y2, route, g, last):
    row = pl.BlockSpec((TM_NORM, D_MODEL), lambda i: (i, 0))
    out_specs = [row] if last else [row, row]
    out_shape = [jax.ShapeDtypeStruct((T, D_MODEL), F32)]
    if not last:
        out_shape.append(jax.ShapeDtypeStruct((T, D_MODEL), BF16))
    return pl.pallas_call(
        functools.partial(_combine_kernel, last),
        grid=(T // TM_NORM,),
        in_specs=[row, row, row,
                  pl.BlockSpec((TM_NORM, ROUTE_LANES), lambda i: (i, 0)),
                  pl.BlockSpec((1, D_MODEL), lambda i: (0, 0))],
        out_specs=out_specs,
        out_shape=out_shape,
        compiler_params=_params("parallel"),
        name="combine_final" if last else "combine",
    )(h, y1, y2, route, g)


def kernel(x, mix_norm_g, w_in, b_gate, conv_dw_w, conv_dw_b, conv_ln_g, conv_ln_b,
           conv_out_w, sgu_ln_g, sgu_ln_b, sgu_ws, sgu_bs, sgu_out_w, attn_rel_bias,
           attn_out_w, w_out, ffn_norm_g, ffn_w_gate, ffn_w_up, ffn_w_down, moe_router,
           moe_w_gate, moe_w_up, moe_w_down, final_norm_g):
    h = x.reshape(T, D_MODEL)
    xn = _norm(h, mix_norm_g[0][None], BF16)
    moe_wg = moe_w_gate.astype(BF16).reshape(-1, D_MODEL, D_FF_EXPERT)
    moe_wu = moe_w_up.astype(BF16).reshape(-1, D_MODEL, D_FF_EXPERT)
    moe_wd = moe_w_down.astype(BF16).reshape(-1, D_FF_EXPERT, D_MODEL)
    for layer in range(DEPTH):
        last = layer == DEPTH - 1
        next_g = (final_norm_g if last else mix_norm_g[layer + 1])[None]
        w_in_l = jnp.concatenate(
            [w_in[layer][:, N_MIX:], w_in[layer][:, :N_MIX]], axis=1).astype(BF16)
        sgu_bias = jnp.repeat(sgu_bs[layer].T, SGU_GW, axis=1)
        gates, qkv, xc, xs = _mix_in(
            xn, w_in_l, conv_dw_w[layer], conv_dw_b[layer][None], conv_ln_g[layer][None],
            conv_ln_b[layer][None], sgu_ln_g[layer][None], sgu_ln_b[layer][None],
            sgu_ws[layer], sgu_bias)
        xa = _attn_branch(qkv, _attn_bias(attn_rel_bias[layer]))

        i = layer // 2
        outs = _merge(h, xc, xs, xa, gates, b_gate[layer][None],
                      conv_out_w[layer].astype(BF16), sgu_out_w[layer].astype(BF16),
                      attn_out_w[layer].astype(BF16), w_out[layer].astype(BF16),
                      ffn_norm_g[layer][None], moe_router[i] if layer % 2 == 1 else None)
        if layer % 2 == 0:
            h, xn2 = outs
            h, xn = _ffn(h, xn2, ffn_w_gate[i].astype(BF16), ffn_w_up[i].astype(BF16),
                         ffn_w_down[i].astype(BF16), next_g)
        else:
            h, xn2, route, counts = outs
            tile_expert, n_tiles, row_token, pos1, pos2 = _route(route, counts)
            ys = _moe(tile_expert + i * N_EXPERTS, n_tiles, xn2, row_token,
                      moe_wg, moe_wu, moe_wd)
            outs = _combine(h, _take_rows(ys, pos1), _take_rows(ys, pos2),
                            route, next_g, last)
            if last:
                h = outs[0]
            else:
                h, xn = outs
    return h.reshape(BATCH, SEQ, D_MODEL)
```

```python
import functools

import jax
import jax.numpy as jnp
from jax import lax
from jax.experimental import pallas as pl
from jax.experimental.pallas import tpu as pltpu

D_MODEL = 1024
BATCH = 4
SEQ = 8192
DEPTH = 4
CHUNK = 64
CONV_W = 512
CONV_K = 31
SGU_W = 512
SGU_GROUPS = 4
SGU_CHUNK = 128
SGU_GW = SGU_W // SGU_GROUPS
ATTN_HEADS = 8
ATTN_HEAD_DIM = 64
ATTN_W = ATTN_HEADS * ATTN_HEAD_DIM
ATTN_LEFT_CHUNKS = 8
REL_MIN = -(CHUNK - 1)
REL_MAX = 128
N_REL = REL_MAX - REL_MIN + 1
N_BRANCH = 3
N_GATE = N_BRANCH * D_MODEL
N_QKV = 3 * ATTN_W
N_MIX = 2 * CONV_W + 2 * SGU_W + N_QKV
N_IN = N_MIX + N_GATE
D_FF = 2816
N_EXPERTS = 8
TOP_K = 2
D_FF_EXPERT = 3584
EPS = 1e-6
NEG_INF = -1e30

T = BATCH * SEQ
F32 = jnp.float32
BF16 = jnp.bfloat16

SUBLANES = 8
LANES = 128

COL_CONV = N_GATE
COL_SGU = COL_CONV + 2 * CONV_W
COL_Q = COL_SGU + 2 * SGU_W

TM_NORM = 1024
TM_MIX = 512
TN_MIX = 512
CONV_HALO = 32
CONV_ROWS = 64
CONV_FIRST_TAP = CONV_HALO - (CONV_K - 1)
TQ_ATTN = 512
ATTN_PREV = ATTN_LEFT_CHUNKS * CHUNK
ATTN_BAND = (ATTN_LEFT_CHUNKS + 1) * CHUNK
ATTN_RVEC = 1024
TM_MERGE = 512
TM_FFN = 512
TF_FFN = 256
TM_MOE = 512
TC_MOE = 256
NT_MOE = (TOP_K * T) // TM_MOE + N_EXPERTS
P_MOE = NT_MOE * TM_MOE
MOE_PARTS = 2
assert NT_MOE % MOE_PARTS == 0
ROUTE_LANES = LANES
R_E1, R_E2, R_W1, R_W2, R_RANK1, R_RANK2 = range(6)

VMEM_LIMIT = 56 * 1024 * 1024


def _params(*sem):
    return pltpu.CompilerParams(dimension_semantics=sem, vmem_limit_bytes=VMEM_LIMIT)


def _resident(shape):
    return pl.BlockSpec(shape, lambda *_: (0,) * len(shape), pipeline_mode=pl.Buffered(1))


def _rms(x, g):
    return x * lax.rsqrt(jnp.mean(x * x, axis=-1, keepdims=True) + EPS) * g


def _layer_norm(x, g, b):
    mu = jnp.mean(x, axis=-1, keepdims=True)
    xc = x - mu
    var = jnp.mean(xc * xc, axis=-1, keepdims=True)
    return xc * lax.rsqrt(var + EPS) * g + b


def _sigmoid(x):
    return 1.0 / (1.0 + jnp.exp(-x))


def _norm_kernel(h_ref, g_ref, o_ref):
    o_ref[...] = _rms(h_ref[...], g_ref[...]).astype(o_ref.dtype)


def _norm(h, g, dtype):
    return pl.pallas_call(
        _norm_kernel,
        grid=(T // TM_NORM,),
        in_specs=[pl.BlockSpec((TM_NORM, D_MODEL), lambda i: (i, 0)),
                  pl.BlockSpec((1, D_MODEL), lambda i: (0, 0))],
        out_specs=pl.BlockSpec((TM_NORM, D_MODEL), lambda i: (i, 0)),
        out_shape=jax.ShapeDtypeStruct((T, D_MODEL), dtype),
        compiler_params=_params("parallel"),
        name="norm",
    )(h, g)


def _mix_in_kernel(x_ref, w_ref, cw_ref, cb_ref, clg_ref, clb_ref, slg_ref, slb_ref, sw_ref,
                   sbias_ref, gate_ref, qkv_ref, xc_ref, xs_ref, buf_ref, shift_ref):
    x = x_ref[...]

    def proj(c0, n):
        return jnp.dot(x, w_ref[:, c0:c0 + n], preferred_element_type=F32)

    ac = proj(COL_CONV, 2 * CONV_W)
    seq_start = (pl.program_id(0) % (SEQ // TM_MIX)) == 0
    buf_ref[0:CONV_HALO, :] = jnp.where(seq_start, 0.0, buf_ref[TM_MIX:TM_MIX + CONV_HALO, :])
    buf_ref[CONV_HALO:, :] = ac[:, :CONV_W] * _sigmoid(ac[:, CONV_W:])
    n_shift = shift_ref.shape[1]
    for s in range(1, SUBLANES):
        shift_ref[s] = buf_ref[pl.ds(s, n_shift), :]

    def conv_block(r0):
        acc = jnp.broadcast_to(cb_ref[...], (CONV_ROWS, CONV_W))
        for k in range(CONV_K):
            base, s = divmod(CONV_FIRST_TAP + k, SUBLANES)
            row = r0 + base * SUBLANES
            if s == 0:
                tap = buf_ref[pl.ds(row, CONV_ROWS), :]
            else:
                tap = shift_ref[s, pl.ds(row, CONV_ROWS), :]
            acc = acc + cw_ref[k:k + 1, :] * tap
        y = _layer_norm(acc, clg_ref[...], clb_ref[...])
        xc_ref[pl.ds(r0, CONV_ROWS), :] = (y * _sigmoid(y)).astype(BF16)

    def sgu():
        a = proj(COL_SGU, 2 * SGU_W)
        a = 0.5 * a * (1.0 + lax.erf(a * (0.5 ** 0.5)))
        u = a[:, :SGU_W]
        v = _layer_norm(a[:, SGU_W:], slg_ref[...], slb_ref[...]).astype(BF16)
        t_half = lax.broadcasted_iota(jnp.int32, (SGU_CHUNK, SGU_CHUNK), 0) // CHUNK
        s_half = lax.broadcasted_iota(jnp.int32, (SGU_CHUNK, SGU_CHUNK), 1) // CHUNK
        causal = s_half <= t_half
        n_chunks = TM_MIX // SGU_CHUNK
        mixed = []
        for g in range(SGU_GROUPS):
            w = jnp.where(causal, sw_ref[g], 0.0).astype(BF16)
            rhs = jnp.concatenate(
                [v[n * SGU_CHUNK:(n + 1) * SGU_CHUNK, g * SGU_GW:(g + 1) * SGU_GW]
                 for n in range(n_chunks)], axis=1)
            mixed.append(jnp.dot(w, rhs, preferred_element_type=F32))
        for n in range(n_chunks):
            m = jnp.concatenate(
                [mixed[g][:, n * SGU_GW:(n + 1) * SGU_GW] for g in range(SGU_GROUPS)], axis=1)
            rows = slice(n * SGU_CHUNK, (n + 1) * SGU_CHUNK)
            xs_ref[rows, :] = (u[rows, :] * (m + sbias_ref[...])).astype(BF16)

    blocks = list(range(0, TM_MIX, CONV_ROWS))
    chunks = ([(gate_ref, c, c) for c in range(0, N_GATE, TN_MIX)]
              + [(qkv_ref, c, COL_Q + c) for c in range(0, N_QKV, TN_MIX)])
    first = len(chunks) - len(blocks) // 2
    for n, (o_ref, oc, wc) in enumerate(chunks):
        o_ref[:, oc:oc + TN_MIX] = proj(wc, TN_MIX).astype(BF16)
        if n == first - 1:
            sgu()
        elif n >= first:
            for r0 in blocks[(n - first) * 2:(n - first + 1) * 2]:
                conv_block(r0)


def _mix_in(xn, w, cw, cb, clg, clb, slg, slb, sw, sbias):
    row = lambda n: pl.BlockSpec((TM_MIX, n), lambda i: (i, 0))
    vec = pl.BlockSpec((1, CONV_W), lambda i: (0, 0))
    n_buf = CONV_HALO + TM_MIX
    return pl.pallas_call(
        _mix_in_kernel,
        grid=(T // TM_MIX,),
        in_specs=[row(D_MODEL), _resident((D_MODEL, N_IN)),
                  pl.BlockSpec((CONV_K, CONV_W), lambda i: (0, 0)), vec, vec, vec, vec, vec,
                  pl.BlockSpec((SGU_GROUPS, SGU_CHUNK, SGU_CHUNK), lambda i: (0, 0, 0)),
                  pl.BlockSpec((SGU_CHUNK, SGU_W), lambda i: (0, 0))],
        out_specs=[row(N_GATE), row(N_QKV), row(CONV_W), row(SGU_W)],
        out_shape=[jax.ShapeDtypeStruct((T, N_GATE), BF16),
                   jax.ShapeDtypeStruct((T, N_QKV), BF16),
                   jax.ShapeDtypeStruct((T, CONV_W), BF16),
                   jax.ShapeDtypeStruct((T, SGU_W), BF16)],
        scratch_shapes=[pltpu.VMEM((n_buf, CONV_W), F32),
                        pltpu.VMEM((SUBLANES, n_buf - SUBLANES, CONV_W), F32)],
        compiler_params=_params("arbitrary"),
        name="mix_in",
    )(xn, w, cw, cb, clg, clb, slg, slb, sw, sbias)


def _attn_bias_kernel(rvec_ref, o_ref):
    x = jnp.broadcast_to(rvec_ref[0], (CHUNK, ATTN_RVEC))
    y = pltpu.roll(x, ATTN_RVEC - (CHUNK - 1), 1, stride=1, stride_axis=0)
    o_ref[0] = y[:, :ATTN_BAND]


def _attn_bias(rel_table):
    n_hi = ATTN_PREV + CHUNK - 1 - REL_MAX + 1
    n_lo = ATTN_RVEC - n_hi - (N_REL - 1)
    rvec = jnp.concatenate([
        jnp.broadcast_to(rel_table[:, N_REL - 1:], (ATTN_HEADS, n_hi)),
        rel_table[:, N_REL - 2:0:-1],
        jnp.broadcast_to(rel_table[:, :1], (ATTN_HEADS, n_lo + 1)),
    ], axis=1).astype(F32)
    return pl.pallas_call(
        _attn_bias_kernel,
        grid=(ATTN_HEADS,),
        in_specs=[pl.BlockSpec((1, 1, ATTN_RVEC), lambda h: (h, 0, 0))],
        out_specs=pl.BlockSpec((1, CHUNK, ATTN_BAND), lambda h: (h, 0, 0)),
        out_shape=jax.ShapeDtypeStruct((ATTN_HEADS, CHUNK, ATTN_BAND), F32),
        compiler_params=_params("parallel"),
        name="attn_bias",
    )(rvec[:, None, :])


def _attn_kernel(q_ref, *refs):
    n_kv = (len(refs) - 2) // 2
    k_refs, v_refs, bias_ref, o_ref = refs[:n_kv], refs[n_kv:2 * n_kv], refs[-2], refs[-1]
    i = pl.program_id(1)
    q = q_ref[...] * (ATTN_HEAD_DIM ** -0.5)
    kwin = jnp.concatenate([r[...] for r in k_refs], axis=0)
    vwin = jnp.concatenate([r[...] for r in v_refs], axis=0)
    lane = lax.broadcasted_iota(jnp.int32, (1, ATTN_BAND), 1)
    n_chunks = TQ_ATTN // CHUNK

    def head(h):
        return slice(h * ATTN_HEAD_DIM, (h + 1) * ATTN_HEAD_DIM)

    def scores(c):
        key_pos = lane + (i * TQ_ATTN - ATTN_PREV + c * CHUNK)
        start_mask = jnp.where(key_pos >= 0, 0.0, NEG_INF)
        rows = slice(c * CHUNK, (c + 1) * CHUNK)
        band = slice(c * CHUNK, c * CHUNK + ATTN_BAND)
        return [lax.dot_general(q[rows, head(h)], kwin[band, head(h)],
                                (((1,), (1,)), ((), ())), preferred_element_type=F32)
                + bias_ref[h] + start_mask for h in range(ATTN_HEADS)]

    def finish(c, ss):
        band = slice(c * CHUNK, c * CHUNK + ATTN_BAND)
        ps = [jnp.exp(s - jnp.max(s, axis=-1, keepdims=True)) for s in ss]
        outs = [jnp.dot(p.astype(BF16), vwin[band, head(h)], preferred_element_type=F32)
                / jnp.sum(p, axis=-1, keepdims=True) for h, p in enumerate(ps)]
        o_ref[c * CHUNK:(c + 1) * CHUNK, :] = jnp.concatenate(outs, axis=1).astype(BF16)

    nxt = scores(0)
    for c in range(n_chunks):
        cur = nxt
        if c + 1 < n_chunks:
            nxt = scores(c + 1)
        finish(c, cur)


def _attn_branch(qkv, bias):
    nq = SEQ // TQ_ATTN
    prev_tiles = ATTN_PREV // TQ_ATTN

    def kv_spec(col, back):
        return pl.BlockSpec(
            (TQ_ATTN, ATTN_W),
            lambda b, i: (b * nq + jnp.maximum(i - back, 0), col))

    return pl.pallas_call(
        _attn_kernel,
        grid=(BATCH, nq),
        in_specs=[pl.BlockSpec((TQ_ATTN, ATTN_W), lambda b, i: (b * nq + i, 0))]
        + [kv_spec(1, back) for back in range(prev_tiles, -1, -1)]
        + [kv_spec(2, back) for back in range(prev_tiles, -1, -1)]
        + [pl.BlockSpec((ATTN_HEADS, CHUNK, ATTN_BAND), lambda b, i: (0, 0, 0))],
        out_specs=pl.BlockSpec((TQ_ATTN, ATTN_W), lambda b, i: (b * nq + i, 0)),
        out_shape=jax.ShapeDtypeStruct((T, ATTN_W), BF16),
        compiler_params=_params("parallel", "parallel"),
        name="attn_branch",
    )(*([qkv] * (3 + 2 * prev_tiles)), bias)


def _merge_kernel(with_router, *refs):
    if with_router:
        (h_ref, xc_ref, xs_ref, xa_ref, gc_ref, gs_ref, ga_ref, bg_ref, wc_ref, ws_ref,
         wa_ref, wo_ref, ng_ref, rcat_ref, ho_ref, xn_ref, route_ref, count_ref) = refs
    else:
        (h_ref, xc_ref, xs_ref, xa_ref, gc_ref, gs_ref, ga_ref, bg_ref, wc_ref, ws_ref,
         wa_ref, wo_ref, ng_ref, ho_ref, xn_ref) = refs

    merged = None
    for n, (x_ref, g_ref, w_ref) in enumerate(
            ((xc_ref, gc_ref, wc_ref), (xs_ref, gs_ref, ws_ref), (xa_ref, ga_ref, wa_ref))):
        y = jnp.dot(x_ref[...], w_ref[...], preferred_element_type=F32)
        gate = _sigmoid(g_ref[...].astype(F32) + bg_ref[:, n * D_MODEL:(n + 1) * D_MODEL])
        merged = gate * y if merged is None else merged + gate * y
    h = h_ref[...] + jnp.dot(merged.astype(BF16), wo_ref[...], preferred_element_type=F32)
    ho_ref[...] = h
    xn = _rms(h, ng_ref[...])
    xn_ref[...] = xn.astype(BF16)

    if with_router:
        x_hi = xn.astype(BF16)
        x_lo = (xn - x_hi.astype(F32)).astype(BF16)
        both = jnp.dot(x_hi, rcat_ref[...], preferred_element_type=F32)
        logits = (both[:, :ROUTE_LANES] + both[:, ROUTE_LANES:]
                  + jnp.dot(x_lo, rcat_ref[:, :ROUTE_LANES], preferred_element_type=F32))
        lane = lax.broadcasted_iota(jnp.int32, logits.shape, 1)
        logits = jnp.where(lane < N_EXPERTS, logits, NEG_INF)
        m1 = jnp.max(logits, axis=-1, keepdims=True)
        i1 = jnp.min(jnp.where(logits == m1, lane, ROUTE_LANES), axis=-1, keepdims=True)
        rest = jnp.where(lane == i1, NEG_INF, logits)
        m2 = jnp.max(rest, axis=-1, keepdims=True)
        i2 = jnp.min(jnp.where(rest == m2, lane, ROUTE_LANES), axis=-1, keepdims=True)
        e2 = jnp.exp(m2 - m1)
        w1 = 1.0 / (1.0 + e2)
        w2 = e2 / (1.0 + e2)

        @pl.when(pl.program_id(0) == 0)
        def _():
            count_ref[...] = jnp.zeros_like(count_ref)

        pick1, pick2 = lane == i1, lane == i2
        picked = jnp.where(pick1 | pick2, 1.0, 0.0)
        r_idx = lax.broadcasted_iota(jnp.int32, (TM_MERGE, TM_MERGE), 0)
        c_idx = lax.broadcasted_iota(jnp.int32, (TM_MERGE, TM_MERGE), 1)
        earlier = jnp.where(c_idx < r_idx, 1.0, 0.0).astype(BF16)
        before = count_ref[0:1, :] + jnp.dot(earlier, picked.astype(BF16),
                                             preferred_element_type=F32)
        rank1 = jnp.sum(jnp.where(pick1, before, 0.0), axis=-1, keepdims=True)
        rank2 = jnp.sum(jnp.where(pick2, before, 0.0), axis=-1, keepdims=True)
        count_ref[...] = jnp.broadcast_to(
            count_ref[0:1, :] + jnp.sum(picked, axis=0, keepdims=True), count_ref.shape)

        record = jnp.zeros_like(logits)
        for lane_id, value in ((R_E1, i1.astype(F32)), (R_E2, i2.astype(F32)), (R_W1, w1),
                               (R_W2, w2), (R_RANK1, rank1), (R_RANK2, rank2)):
            record = jnp.where(lane == lane_id, value, record)
        route_ref[...] = record


def _merge(h, xc, xs, xa, gates, b_gate, wc, ws, wa, wo, norm_g, router=None):
    with_router = router is not None
    row = lambda w: pl.BlockSpec((TM_MERGE, w), lambda i: (i, 0))
    gate = lambda n: pl.BlockSpec((TM_MERGE, D_MODEL), lambda i: (i, n))
    full = lambda r, c: pl.BlockSpec((r, c), lambda i: (0, 0))
    in_specs = [row(D_MODEL), row(CONV_W), row(SGU_W), row(ATTN_W),
                gate(0), gate(1), gate(2), full(1, N_GATE),
                full(CONV_W, D_MODEL), full(SGU_W, D_MODEL), full(ATTN_W, D_MODEL),
                full(D_MODEL, D_MODEL), full(1, D_MODEL)]
    args = [h, xc, xs, xa, gates, gates, gates, b_gate, wc, ws, wa, wo, norm_g]
    out_specs = [row(D_MODEL), row(D_MODEL)]
    out_shape = [jax.ShapeDtypeStruct((T, D_MODEL), F32),
                 jax.ShapeDtypeStruct((T, D_MODEL), BF16)]
    if with_router:
        r = jnp.pad(router, ((0, 0), (0, ROUTE_LANES - N_EXPERTS)))
        r_hi = r.astype(BF16)
        r_lo = (r - r_hi.astype(F32)).astype(BF16)
        in_specs.append(full(D_MODEL, 2 * ROUTE_LANES))
        args.append(jnp.concatenate([r_hi, r_lo], axis=1))
        out_specs += [row(ROUTE_LANES), full(SUBLANES, ROUTE_LANES)]
        out_shape += [jax.ShapeDtypeStruct((T, ROUTE_LANES), F32),
                      jax.ShapeDtypeStruct((SUBLANES, ROUTE_LANES), F32)]
    return pl.pallas_call(
        functools.partial(_merge_kernel, with_router),
        grid=(T // TM_MERGE,),
        in_specs=in_specs,
        out_specs=out_specs,
        out_shape=out_shape,
        compiler_params=_params("arbitrary" if with_router else "parallel"),
        name="merge_router" if with_router else "merge",
    )(*args)


def _swiglu_chunk(x, wg, wu, wd):
    g = jnp.dot(x, wg, preferred_element_type=F32)
    u = jnp.dot(x, wu, preferred_element_type=F32)
    act = (g * _sigmoid(g) * u).astype(BF16)
    return jnp.dot(act, wd, preferred_element_type=F32)


def _ffn_kernel(h_ref, x_ref, wg_ref, wu_ref, wd_ref, ng_ref, ho_ref, xn_ref):
    x = x_ref[...]
    acc = h_ref[...]
    for c in range(D_FF // TF_FFN):
        cols = slice(c * TF_FFN, (c + 1) * TF_FFN)
        acc = acc + _swiglu_chunk(x, wg_ref[:, cols], wu_ref[:, cols], wd_ref[cols, :])
    ho_ref[...] = acc
    xn_ref[...] = _rms(acc, ng_ref[...]).astype(BF16)


def _ffn(h, xn, wg, wu, wd, next_g):
    row = pl.BlockSpec((TM_FFN, D_MODEL), lambda i: (i, 0))
    return pl.pallas_call(
        _ffn_kernel,
        grid=(T // TM_FFN,),
        in_specs=[row, row,
                  _resident((D_MODEL, D_FF)), _resident((D_MODEL, D_FF)),
                  _resident((D_FF, D_MODEL)),
                  pl.BlockSpec((1, D_MODEL), lambda i: (0, 0))],
        out_specs=[row, row],
        out_shape=[jax.ShapeDtypeStruct((T, D_MODEL), F32),
                   jax.ShapeDtypeStruct((T, D_MODEL), BF16)],
        compiler_params=_params("parallel"),
        name="ffn",
    )(h, xn, wg, wu, wd, next_g)


def _moe_kernel(first_tile, tile_expert_ref, n_tiles_ref, x_ref, wg_ref, wu_ref, wd_ref,
                *rest):
    o_ref = rest[-1]

    @pl.when(first_tile + pl.program_id(0) < n_tiles_ref[0])
    def _():
        x = x_ref[...]
        y = None
        for c in range(D_FF_EXPERT // TC_MOE):
            cols = slice(c * TC_MOE, (c + 1) * TC_MOE)
            part = _swiglu_chunk(x, wg_ref[0, :, cols], wu_ref[0, :, cols], wd_ref[0, cols, :])
            y = part if y is None else y + part
        o_ref[...] = y.astype(BF16)


def _moe_part(tile_expert, n_tiles, xs, wg, wu, wd, ys, first_tile):
    w_spec = lambda r, c: pl.BlockSpec(
        (1, r, c), lambda i, te, nt: (te[first_tile + i], 0, 0))
    in_specs = [pl.BlockSpec((TM_MOE, D_MODEL), lambda i, te, nt: (i, 0)),
                w_spec(D_MODEL, D_FF_EXPERT), w_spec(D_MODEL, D_FF_EXPERT),
                w_spec(D_FF_EXPERT, D_MODEL)]
    args = [tile_expert, n_tiles, xs, wg, wu, wd]
    aliases = {}
    if ys is not None:
        in_specs.append(pl.BlockSpec(memory_space=pl.ANY))
        aliases = {len(args): 0}
        args.append(ys)
    grid_spec = pltpu.PrefetchScalarGridSpec(
        num_scalar_prefetch=2,
        grid=(xs.shape[0] // TM_MOE,),
        in_specs=in_specs,
        out_specs=pl.BlockSpec((TM_MOE, D_MODEL), lambda i, te, nt: (first_tile + i, 0)),
    )
    return pl.pallas_call(
        functools.partial(_moe_kernel, first_tile),
        grid_spec=grid_spec,
        out_shape=jax.ShapeDtypeStruct((P_MOE, D_MODEL), BF16),
        input_output_aliases=aliases,
        compiler_params=_params("arbitrary"),
        name="moe_experts",
    )(*args)


def _moe(tile_expert, n_tiles, xn, row_token, wg, wu, wd):
    part_rows = P_MOE // MOE_PARTS
    ys = None
    for p in range(MOE_PARTS):
        xs = _take_rows(xn, row_token[p * part_rows:(p + 1) * part_rows])
        ys = _moe_part(tile_expert, n_tiles, xs, wg, wu, wd, ys, p * (NT_MOE // MOE_PARTS))
    return ys


def _take_rows(x, rows):
    return x.at[rows].get(mode="promise_in_bounds")


def _route(route, counts):
    counts = counts[0, :N_EXPERTS].astype(jnp.int32)
    tiles = (counts + TM_MOE - 1) // TM_MOE
    tile_end = jnp.cumsum(tiles)
    row_start = (tile_end - tiles) * TM_MOE

    def position(expert_lane, rank_lane):
        expert = route[:, expert_lane].astype(jnp.int32)
        start = sum(jnp.where(expert == e, row_start[e], 0) for e in range(N_EXPERTS))
        return start + route[:, rank_lane].astype(jnp.int32)

    pos1, pos2 = position(R_E1, R_RANK1), position(R_E2, R_RANK2)
    tile_expert = jnp.minimum(
        jnp.sum(jnp.arange(NT_MOE, dtype=jnp.int32)[:, None] >= tile_end[None, :], axis=1),
        N_EXPERTS - 1)
    pad = tiles * TM_MOE - counts
    pad_end = jnp.cumsum(pad)
    k = jnp.arange(P_MOE - TOP_K * T, dtype=jnp.int32)
    lanes = jnp.arange(N_EXPERTS, dtype=jnp.int32)[None, :]
    filler_expert = jnp.sum(k[:, None] >= pad_end[None, :], axis=1)
    filler_onehot = (filler_expert[:, None] == lanes).astype(jnp.int32)
    in_expert = jnp.sum(filler_onehot * (row_start + counts - (pad_end - pad))[None, :], axis=1) + k
    in_tail = tile_end[-1] * TM_MOE + (k - pad_end[-1])
    filler_pos = jnp.where(filler_expert < N_EXPERTS, in_expert, in_tail)
    token = jnp.arange(T, dtype=jnp.int32)
    _, row_token = lax.sort(
        (jnp.concatenate([pos1, pos2, filler_pos]),
         jnp.concatenate([token, token, jnp.zeros_like(k)])),
        num_keys=1)
    return (tile_expert.astype(jnp.int32), tile_end[-1:].astype(jnp.int32), row_token,
            pos1, pos2)


def _combine_kernel(last, h_ref, y1_ref, y2_ref, route_ref, g_ref, *out_refs):
    w1 = route_ref[:, R_W1:R_W1 + 1]
    w2 = route_ref[:, R_W2:R_W2 + 1]
    h = h_ref[...] + w1 * y1_ref[...].astype(F32) + w2 * y2_ref[...].astype(F32)
    if last:
        out_refs[0][...] = _rms(h, g_ref[...])
    else:
        out_refs[0][...] = h
        out_refs[1][...] = _rms(h, g_ref[...]).astype(BF16)


def _combine(h, y1, y2, route, g, last):
    row = pl.BlockSpec((TM_NORM, D_MODEL), lambda i: (i, 0))
    out_specs = [row] if last else [row, row]
    out_shape = [jax.ShapeDtypeStruct((T, D_MODEL), F32)]
    if not last:
        out_shape.append(jax.ShapeDtypeStruct((T, D_MODEL), BF16))
    return pl.pallas_call(
        functools.partial(_combine_kernel, last),
        grid=(T // TM_NORM,),
        in_specs=[row, row, row,
                  pl.BlockSpec((TM_NORM, ROUTE_LANES), lambda i: (i, 0)),
                  pl.BlockSpec((1, D_MODEL), lambda i: (0, 0))],
        out_specs=out_specs,
        out_shape=out_shape,
        compiler_params=_params("parallel"),
        name="combine_final" if last else "combine",
    )(h, y1, y2, route, g)


def kernel(x, mix_norm_g, w_in, b_gate, conv_dw_w, conv_dw_b, conv_ln_g, conv_ln_b,
           conv_out_w, sgu_ln_g, sgu_ln_b, sgu_ws, sgu_bs, sgu_out_w, attn_rel_bias,
           attn_out_w, w_out, ffn_norm_g, ffn_w_gate, ffn_w_up, ffn_w_down, moe_router,
           moe_w_gate, moe_w_up, moe_w_down, final_norm_g):
    h = x.reshape(T, D_MODEL)
    xn = _norm(h, mix_norm_g[0][None], BF16)
    moe_wg = moe_w_gate.astype(BF16).reshape(-1, D_MODEL, D_FF_EXPERT)
    moe_wu = moe_w_up.astype(BF16).reshape(-1, D_MODEL, D_FF_EXPERT)
    moe_wd = moe_w_down.astype(BF16).reshape(-1, D_FF_EXPERT, D_MODEL)
    for layer in range(DEPTH):
        last = layer == DEPTH - 1
        next_g = (final_norm_g if last else mix_norm_g[layer + 1])[None]
        w_in_l = jnp.concatenate(
            [w_in[layer][:, N_MIX:], w_in[layer][:, :N_MIX]], axis=1).astype(BF16)
        sgu_bias = jnp.repeat(sgu_bs[layer].T, SGU_GW, axis=1)
        gates, qkv, xc, xs = _mix_in(
            xn, w_in_l, conv_dw_w[layer], conv_dw_b[layer][None], conv_ln_g[layer][None],
            conv_ln_b[layer][None], sgu_ln_g[layer][None], sgu_ln_b[layer][None],
            sgu_ws[layer], sgu_bias)
        xa = _attn_branch(qkv, _attn_bias(attn_rel_bias[layer]))

        i = layer // 2
        outs = _merge(h, xc, xs, xa, gates, b_gate[layer][None],
                      conv_out_w[layer].astype(BF16), sgu_out_w[layer].astype(BF16),
                      attn_out_w[layer].astype(BF16), w_out[layer].astype(BF16),
                      ffn_norm_g[layer][None], moe_router[i] if layer % 2 == 1 else None)
        if layer % 2 == 0:
            h, xn2 = outs
            h, xn = _ffn(h, xn2, ffn_w_gate[i].astype(BF16), ffn_w_up[i].astype(BF16),
                         ffn_w_down[i].astype(BF16), next_g)
        else:
            h, xn2, route, counts = outs
            tile_expert, n_tiles, row_token, pos1, pos2 = _route(route, counts)
            ys = _moe(tile_expert + i * N_EXPERTS, n_tiles, xn2, row_token,
                      moe_wg, moe_wu, moe_wd)
            outs = _combine(h, _take_rows(ys, pos1), _take_rows(ys, pos2),
                            route, next_g, last)
            if last:
                h = outs[0]
            else:
                h, xn = outs
    return h.reshape(BATCH, SEQ, D_MODEL)
```

```python
import functools

import jax
import jax.numpy as jnp
from jax import lax
from jax.experimental import pallas as pl
from jax.experimental.pallas import tpu as pltpu

D_MODEL = 1024
BATCH = 4
SEQ = 8192
DEPTH = 4
CHUNK = 64
CONV_W = 512
CONV_K = 31
SGU_W = 512
SGU_GROUPS = 4
SGU_CHUNK = 128
SGU_GW = SGU_W // SGU_GROUPS
ATTN_HEADS = 8
ATTN_HEAD_DIM = 64
ATTN_W = ATTN_HEADS * ATTN_HEAD_DIM
ATTN_LEFT_CHUNKS = 8
REL_MIN = -(CHUNK - 1)
REL_MAX = 128
N_REL = REL_MAX - REL_MIN + 1
N_BRANCH = 3
N_GATE = N_BRANCH * D_MODEL
N_QKV = 3 * ATTN_W
N_MIX = 2 * CONV_W + 2 * SGU_W + N_QKV
N_IN = N_MIX + N_GATE
D_FF = 2816
N_EXPERTS = 8
TOP_K = 2
D_FF_EXPERT = 3584
EPS = 1e-6
NEG_INF = -1e30

T = BATCH * SEQ
F32 = jnp.float32
BF16 = jnp.bfloat16

SUBLANES = 8
LANES = 128

COL_CONV = N_GATE
COL_SGU = COL_CONV + 2 * CONV_W
COL_Q = COL_SGU + 2 * SGU_W

TM_NORM = 1024
TM_MIX = 512
TN_MIX = 512
CONV_HALO = 32
CONV_ROWS = 64
CONV_FIRST_TAP = CONV_HALO - (CONV_K - 1)
TQ_ATTN = 512
ATTN_PREV = ATTN_LEFT_CHUNKS * CHUNK
ATTN_BAND = (ATTN_LEFT_CHUNKS + 1) * CHUNK
ATTN_RVEC = 1024
TM_MERGE = 512
TM_FFN = 512
TF_FFN = 256
TM_MOE = 512
TC_MOE = 256
NT_MOE = (TOP_K * T) // TM_MOE + N_EXPERTS
P_MOE = NT_MOE * TM_MOE
MOE_PARTS = 2
assert NT_MOE % MOE_PARTS == 0
ROUTE_LANES = LANES
R_E1, R_E2, R_W1, R_W2, R_RANK1, R_RANK2 = range(6)

VMEM_LIMIT = 56 * 1024 * 1024


def _params(*sem):
    return pltpu.CompilerParams(dimension_semantics=sem, vmem_limit_bytes=VMEM_LIMIT)


def _resident(shape):
    return pl.BlockSpec(shape, lambda *_: (0,) * len(shape), pipeline_mode=pl.Buffered(1))


def _rms(x, g):
    return x * lax.rsqrt(jnp.mean(x * x, axis=-1, keepdims=True) + EPS) * g


def _layer_norm(x, g, b):
    mu = jnp.mean(x, axis=-1, keepdims=True)
    xc = x - mu
    var = jnp.mean(xc * xc, axis=-1, keepdims=True)
    return xc * lax.rsqrt(var + EPS) * g + b


def _sigmoid(x):
    return 0.5 * jnp.tanh(0.5 * x) + 0.5


def _norm_kernel(h_ref, g_ref, o_ref):
    o_ref[...] = _rms(h_ref[...], g_ref[...]).astype(o_ref.dtype)


def _norm(h, g, dtype):
    return pl.pallas_call(
        _norm_kernel,
        grid=(T // TM_NORM,),
        in_specs=[pl.BlockSpec((TM_NORM, D_MODEL), lambda i: (i, 0)),
                  pl.BlockSpec((1, D_MODEL), lambda i: (0, 0))],
        out_specs=pl.BlockSpec((TM_NORM, D_MODEL), lambda i: (i, 0)),
        out_shape=jax.ShapeDtypeStruct((T, D_MODEL), dtype),
        compiler_params=_params("parallel"),
        name="norm",
    )(h, g)


def _mix_in_kernel(x_ref, w_ref, cw_ref, cb_ref, clg_ref, clb_ref, slg_ref, slb_ref, sw_ref,
                   sbias_ref, gate_ref, qkv_ref, xc_ref, xs_ref, buf_ref, shift_ref):
    x = x_ref[...]

    def proj(c0, n):
        return jnp.dot(x, w_ref[:, c0:c0 + n], preferred_element_type=F32)

    ac = proj(COL_CONV, 2 * CONV_W)
    seq_start = (pl.program_id(0) % (SEQ // TM_MIX)) == 0
    buf_ref[0:CONV_HALO, :] = jnp.where(seq_start, 0.0, buf_ref[TM_MIX:TM_MIX + CONV_HALO, :])
    buf_ref[CONV_HALO:, :] = ac[:, :CONV_W] * _sigmoid(ac[:, CONV_W:])
    n_shift = shift_ref.shape[1]
    for s in range(1, SUBLANES):
        shift_ref[s] = buf_ref[pl.ds(s, n_shift), :]

    def conv_block(r0):
        acc = jnp.broadcast_to(cb_ref[...], (CONV_ROWS, CONV_W))
        for k in range(CONV_K):
            base, s = divmod(CONV_FIRST_TAP + k, SUBLANES)
            row = r0 + base * SUBLANES
            if s == 0:
                tap = buf_ref[pl.ds(row, CONV_ROWS), :]
            else:
                tap = shift_ref[s, pl.ds(row, CONV_ROWS), :]
            acc = acc + cw_ref[k:k + 1, :] * tap
        y = _layer_norm(acc, clg_ref[...], clb_ref[...])
        xc_ref[pl.ds(r0, CONV_ROWS), :] = (y * _sigmoid(y)).astype(BF16)

    def sgu():
        a = proj(COL_SGU, 2 * SGU_W)
        a = 0.5 * a * (1.0 + lax.erf(a * (0.5 ** 0.5)))
        u = a[:, :SGU_W]
        v = _layer_norm(a[:, SGU_W:], slg_ref[...], slb_ref[...]).astype(BF16)
        t_half = lax.broadcasted_iota(jnp.int32, (SGU_CHUNK, SGU_CHUNK), 0) // CHUNK
        s_half = lax.broadcasted_iota(jnp.int32, (SGU_CHUNK, SGU_CHUNK), 1) // CHUNK
        causal = s_half <= t_half
        n_chunks = TM_MIX // SGU_CHUNK
        mixed = []
        for g in range(SGU_GROUPS):
            w = jnp.where(causal, sw_ref[g], 0.0).astype(BF16)
            rhs = jnp.concatenate(
                [v[n * SGU_CHUNK:(n + 1) * SGU_CHUNK, g * SGU_GW:(g + 1) * SGU_GW]
                 for n in range(n_chunks)], axis=1)
            mixed.append(jnp.dot(w, rhs, preferred_element_type=F32))
        for n in range(n_chunks):
            m = jnp.concatenate(
                [mixed[g][:, n * SGU_GW:(n + 1) * SGU_GW] for g in range(SGU_GROUPS)], axis=1)
            rows = slice(n * SGU_CHUNK, (n + 1) * SGU_CHUNK)
            xs_ref[rows, :] = (u[rows, :] * (m + sbias_ref[...])).astype(BF16)

    blocks = list(range(0, TM_MIX, CONV_ROWS))
    chunks = ([(gate_ref, c, c) for c in range(0, N_GATE, TN_MIX)]
              + [(qkv_ref, c, COL_Q + c) for c in range(0, N_QKV, TN_MIX)])
    first = len(chunks) - len(blocks) // 2
    for n, (o_ref, oc, wc) in enumerate(chunks):
        o_ref[:, oc:oc + TN_MIX] = proj(wc, TN_MIX).astype(BF16)
        if n == first - 1:
            sgu()
        elif n >= first:
            for r0 in blocks[(n - first) * 2:(n - first + 1) * 2]:
                conv_block(r0)


def _mix_in(xn, w, cw, cb, clg, clb, slg, slb, sw, sbias):
    row = lambda n: pl.BlockSpec((TM_MIX, n), lambda i: (i, 0))
    vec = pl.BlockSpec((1, CONV_W), lambda i: (0, 0))
    n_buf = CONV_HALO + TM_MIX
    return pl.pallas_call(
        _mix_in_kernel,
        grid=(T // TM_MIX,),
        in_specs=[row(D_MODEL), _resident((D_MODEL, N_IN)),
                  pl.BlockSpec((CONV_K, CONV_W), lambda i: (0, 0)), vec, vec, vec, vec, vec,
                  pl.BlockSpec((SGU_GROUPS, SGU_CHUNK, SGU_CHUNK), lambda i: (0, 0, 0)),
                  pl.BlockSpec((SGU_CHUNK, SGU_W), lambda i: (0, 0))],
        out_specs=[row(N_GATE), row(N_QKV), row(CONV_W), row(SGU_W)],
        out_shape=[jax.ShapeDtypeStruct((T, N_GATE), BF16),
                   jax.ShapeDtypeStruct((T, N_QKV), BF16),
                   jax.ShapeDtypeStruct((T, CONV_W), BF16),
                   jax.ShapeDtypeStruct((T, SGU_W), BF16)],
        scratch_shapes=[pltpu.VMEM((n_buf, CONV_W), F32),
                        pltpu.VMEM((SUBLANES, n_buf - SUBLANES, CONV_W), F32)],
        compiler_params=_params("arbitrary"),
        name="mix_in",
    )(xn, w, cw, cb, clg, clb, slg, slb, sw, sbias)


def _attn_bias_kernel(rvec_ref, o_ref):
    x = jnp.broadcast_to(rvec_ref[0], (CHUNK, ATTN_RVEC))
    y = pltpu.roll(x, ATTN_RVEC - (CHUNK - 1), 1, stride=1, stride_axis=0)
    o_ref[0] = y[:, :ATTN_BAND]


def _attn_bias(rel_table):
    n_hi = ATTN_PREV + CHUNK - 1 - REL_MAX + 1
    n_lo = ATTN_RVEC - n_hi - (N_REL - 1)
    rvec = jnp.concatenate([
        jnp.broadcast_to(rel_table[:, N_REL - 1:], (ATTN_HEADS, n_hi)),
        rel_table[:, N_REL - 2:0:-1],
        jnp.broadcast_to(rel_table[:, :1], (ATTN_HEADS, n_lo + 1)),
    ], axis=1).astype(F32)
    return pl.pallas_call(
        _attn_bias_kernel,
        grid=(ATTN_HEADS,),
        in_specs=[pl.BlockSpec((1, 1, ATTN_RVEC), lambda h: (h, 0, 0))],
        out_specs=pl.BlockSpec((1, CHUNK, ATTN_BAND), lambda h: (h, 0, 0)),
        out_shape=jax.ShapeDtypeStruct((ATTN_HEADS, CHUNK, ATTN_BAND), F32),
        compiler_params=_params("parallel"),
        name="attn_bias",
    )(rvec[:, None, :])


def _attn_kernel(q_ref, *refs):
    n_kv = (len(refs) - 2) // 2
    k_refs, v_refs, bias_ref, o_ref = refs[:n_kv], refs[n_kv:2 * n_kv], refs[-2], refs[-1]
    i = pl.program_id(1)
    q = q_ref[...] * (ATTN_HEAD_DIM ** -0.5)
    kwin = jnp.concatenate([r[...] for r in k_refs], axis=0)
    vwin = jnp.concatenate([r[...] for r in v_refs], axis=0)
    lane = lax.broadcasted_iota(jnp.int32, (1, ATTN_BAND), 1)
    n_chunks = TQ_ATTN // CHUNK

    def head(h):
        return slice(h * ATTN_HEAD_DIM, (h + 1) * ATTN_HEAD_DIM)

    def scores(c):
        key_pos = lane + (i * TQ_ATTN - ATTN_PREV + c * CHUNK)
        start_mask = jnp.where(key_pos >= 0, 0.0, NEG_INF)
        rows = slice(c * CHUNK, (c + 1) * CHUNK)
        band = slice(c * CHUNK, c * CHUNK + ATTN_BAND)
        return [lax.dot_general(q[rows, head(h)], kwin[band, head(h)],
                                (((1,), (1,)), ((), ())), preferred_element_type=F32)
                + bias_ref[h] + start_mask for h in range(ATTN_HEADS)]

    def finish(c, ss):
        band = slice(c * CHUNK, c * CHUNK + ATTN_BAND)
        ps = [jnp.exp(s - jnp.max(s, axis=-1, keepdims=True)) for s in ss]
        outs = [jnp.dot(p.astype(BF16), vwin[band, head(h)], preferred_element_type=F32)
                / jnp.sum(p, axis=-1, keepdims=True) for h, p in enumerate(ps)]
        o_ref[c * CHUNK:(c + 1) * CHUNK, :] = jnp.concatenate(outs, axis=1).astype(BF16)

    nxt = scores(0)
    for c in range(n_chunks):
        cur = nxt
        if c + 1 < n_chunks:
            nxt = scores(c + 1)
        finish(c, cur)


def _attn_branch(qkv, bias):
    nq = SEQ // TQ_ATTN
    prev_tiles = ATTN_PREV // TQ_ATTN

    def kv_spec(col, back):
        return pl.BlockSpec(
            (TQ_ATTN, ATTN_W),
            lambda b, i: (b * nq + jnp.maximum(i - back, 0), col))

    return pl.pallas_call(
        _attn_kernel,
        grid=(BATCH, nq),
        in_specs=[pl.BlockSpec((TQ_ATTN, ATTN_W), lambda b, i: (b * nq + i, 0))]
        + [kv_spec(1, back) for back in range(prev_tiles, -1, -1)]
        + [kv_spec(2, back) for back in range(prev_tiles, -1, -1)]
        + [pl.BlockSpec((ATTN_HEADS, CHUNK, ATTN_BAND), lambda b, i: (0, 0, 0))],
        out_specs=pl.BlockSpec((TQ_ATTN, ATTN_W), lambda b, i: (b * nq + i, 0)),
        out_shape=jax.ShapeDtypeStruct((T, ATTN_W), BF16),
        compiler_params=_params("parallel", "parallel"),
        name="attn_branch",
    )(*([qkv] * (3 + 2 * prev_tiles)), bias)


def _merge_kernel(with_router, *refs):
    if with_router:
        (h_ref, xc_ref, xs_ref, xa_ref, gc_ref, gs_ref, ga_ref, bg_ref, wc_ref, ws_ref,
         wa_ref, wo_ref, ng_ref, rcat_ref, ho_ref, xn_ref, route_ref, count_ref) = refs
    else:
        (h_ref, xc_ref, xs_ref, xa_ref, gc_ref, gs_ref, ga_ref, bg_ref, wc_ref, ws_ref,
         wa_ref, wo_ref, ng_ref, ho_ref, xn_ref) = refs

    merged = None
    for n, (x_ref, g_ref, w_ref) in enumerate(
            ((xc_ref, gc_ref, wc_ref), (xs_ref, gs_ref, ws_ref), (xa_ref, ga_ref, wa_ref))):
        y = jnp.dot(x_ref[...], w_ref[...], preferred_element_type=F32)
        gate = _sigmoid(g_ref[...].astype(F32) + bg_ref[:, n * D_MODEL:(n + 1) * D_MODEL])
        merged = gate * y if merged is None else merged + gate * y
    h = h_ref[...] + jnp.dot(merged.astype(BF16), wo_ref[...], preferred_element_type=F32)
    ho_ref[...] = h
    xn = _rms(h, ng_ref[...])
    xn_ref[...] = xn.astype(BF16)

    if with_router:
        x_hi = xn.astype(BF16)
        x_lo = (xn - x_hi.astype(F32)).astype(BF16)
        both = jnp.dot(x_hi, rcat_ref[...], preferred_element_type=F32)
        logits = (both[:, :ROUTE_LANES] + both[:, ROUTE_LANES:]
                  + jnp.dot(x_lo, rcat_ref[:, :ROUTE_LANES], preferred_element_type=F32))
        lane = lax.broadcasted_iota(jnp.int32, logits.shape, 1)
        logits = jnp.where(lane < N_EXPERTS, logits, NEG_INF)
        m1 = jnp.max(logits, axis=-1, keepdims=True)
        i1 = jnp.min(jnp.where(logits == m1, lane, ROUTE_LANES), axis=-1, keepdims=True)
        rest = jnp.where(lane == i1, NEG_INF, logits)
        m2 = jnp.max(rest, axis=-1, keepdims=True)
        i2 = jnp.min(jnp.where(rest == m2, lane, ROUTE_LANES), axis=-1, keepdims=True)
        e2 = jnp.exp(m2 - m1)
        w1 = 1.0 / (1.0 + e2)
        w2 = e2 / (1.0 + e2)

        @pl.when(pl.program_id(0) == 0)
        def _():
            count_ref[...] = jnp.zeros_like(count_ref)

        pick1, pick2 = lane == i1, lane == i2
        picked = jnp.where(pick1 | pick2, 1.0, 0.0)
        r_idx = lax.broadcasted_iota(jnp.int32, (TM_MERGE, TM_MERGE), 0)
        c_idx = lax.broadcasted_iota(jnp.int32, (TM_MERGE, TM_MERGE), 1)
        earlier = jnp.where(c_idx < r_idx, 1.0, 0.0).astype(BF16)
        before = count_ref[0:1, :] + jnp.dot(earlier, picked.astype(BF16),
                                             preferred_element_type=F32)
        rank1 = jnp.sum(jnp.where(pick1, before, 0.0), axis=-1, keepdims=True)
        rank2 = jnp.sum(jnp.where(pick2, before, 0.0), axis=-1, keepdims=True)
        count_ref[...] = jnp.broadcast_to(
            count_ref[0:1, :] + jnp.sum(picked, axis=0, keepdims=True), count_ref.shape)

        record = jnp.zeros_like(logits)
        for lane_id, value in ((R_E1, i1.astype(F32)), (R_E2, i2.astype(F32)), (R_W1, w1),
                               (R_W2, w2), (R_RANK1, rank1), (R_RANK2, rank2)):
            record = jnp.where(lane == lane_id, value, record)
        route_ref[...] = record


def _merge(h, xc, xs, xa, gates, b_gate, wc, ws, wa, wo, norm_g, router=None):
    with_router = router is not None
    row = lambda w: pl.BlockSpec((TM_MERGE, w), lambda i: (i, 0))
    gate = lambda n: pl.BlockSpec((TM_MERGE, D_MODEL), lambda i: (i, n))
    full = lambda r, c: pl.BlockSpec((r, c), lambda i: (0, 0))
    in_specs = [row(D_MODEL), row(CONV_W), row(SGU_W), row(ATTN_W),
                gate(0), gate(1), gate(2), full(1, N_GATE),
                full(CONV_W, D_MODEL), full(SGU_W, D_MODEL), full(ATTN_W, D_MODEL),
                full(D_MODEL, D_MODEL), full(1, D_MODEL)]
    args = [h, xc, xs, xa, gates, gates, gates, b_gate, wc, ws, wa, wo, norm_g]
    out_specs = [row(D_MODEL), row(D_MODEL)]
    out_shape = [jax.ShapeDtypeStruct((T, D_MODEL), F32),
                 jax.ShapeDtypeStruct((T, D_MODEL), BF16)]
    if with_router:
        r = jnp.pad(router, ((0, 0), (0, ROUTE_LANES - N_EXPERTS)))
        r_hi = r.astype(BF16)
        r_lo = (r - r_hi.astype(F32)).astype(BF16)
        in_specs.append(full(D_MODEL, 2 * ROUTE_LANES))
        args.append(jnp.concatenate([r_hi, r_lo], axis=1))
        out_specs += [row(ROUTE_LANES), full(SUBLANES, ROUTE_LANES)]
        out_shape += [jax.ShapeDtypeStruct((T, ROUTE_LANES), F32),
                      jax.ShapeDtypeStruct((SUBLANES, ROUTE_LANES), F32)]
    return pl.pallas_call(
        functools.partial(_merge_kernel, with_router),
        grid=(T // TM_MERGE,),
        in_specs=in_specs,
        out_specs=out_specs,
        out_shape=out_shape,
        compiler_params=_params("arbitrary" if with_router else "parallel"),
        name="merge_router" if with_router else "merge",
    )(*args)


def _swiglu_chunk(x, wg, wu, wd):
    g = jnp.dot(x, wg, preferred_element_type=F32)
    u = jnp.dot(x, wu, preferred_element_type=F32)
    act = (g * _sigmoid(g) * u).astype(BF16)
    return jnp.dot(act, wd, preferred_element_type=F32)


def _ffn_kernel(h_ref, x_ref, wg_ref, wu_ref, wd_ref, ng_ref, ho_ref, xn_ref):
    x = x_ref[...]
    acc = h_ref[...]
    for c in range(D_FF // TF_FFN):
        cols = slice(c * TF_FFN, (c + 1) * TF_FFN)
        acc = acc + _swiglu_chunk(x, wg_ref[:, cols], wu_ref[:, cols], wd_ref[cols, :])
    ho_ref[...] = acc
    xn_ref[...] = _rms(acc, ng_ref[...]).astype(BF16)


def _ffn(h, xn, wg, wu, wd, next_g):
    row = pl.BlockSpec((TM_FFN, D_MODEL), lambda i: (i, 0))
    return pl.pallas_call(
        _ffn_kernel,
        grid=(T // TM_FFN,),
        in_specs=[row, row,
                  _resident((D_MODEL, D_FF)), _resident((D_MODEL, D_FF)),
                  _resident((D_FF, D_MODEL)),
                  pl.BlockSpec((1, D_MODEL), lambda i: (0, 0))],
        out_specs=[row, row],
        out_shape=[jax.ShapeDtypeStruct((T, D_MODEL), F32),
                   jax.ShapeDtypeStruct((T, D_MODEL), BF16)],
        compiler_params=_params("parallel"),
        name="ffn",
    )(h, xn, wg, wu, wd, next_g)


def _moe_kernel(first_tile, tile_expert_ref, n_tiles_ref, x_ref, wg_ref, wu_ref, wd_ref,
                *rest):
    o_ref = rest[-1]

    @pl.when(first_tile + pl.program_id(0) < n_tiles_ref[0])
    def _():
        x = x_ref[...]
        y = None
        for c in range(D_FF_EXPERT // TC_MOE):
            cols = slice(c * TC_MOE, (c + 1) * TC_MOE)
            part = _swiglu_chunk(x, wg_ref[0, :, cols], wu_ref[0, :, cols], wd_ref[0, cols, :])
            y = part if y is None else y + part
        o_ref[...] = y.astype(BF16)


def _moe_part(tile_expert, n_tiles, xs, wg, wu, wd, ys, first_tile):
    w_spec = lambda r, c: pl.BlockSpec(
        (1, r, c), lambda i, te, nt: (te[first_tile + i], 0, 0))
    in_specs = [pl.BlockSpec((TM_MOE, D_MODEL), lambda i, te, nt: (i, 0)),
                w_spec(D_MODEL, D_FF_EXPERT), w_spec(D_MODEL, D_FF_EXPERT),
                w_spec(D_FF_EXPERT, D_MODEL)]
    args = [tile_expert, n_tiles, xs, wg, wu, wd]
    aliases = {}
    if ys is not None:
        in_specs.append(pl.BlockSpec(memory_space=pl.ANY))
        aliases = {len(args): 0}
        args.append(ys)
    grid_spec = pltpu.PrefetchScalarGridSpec(
        num_scalar_prefetch=2,
        grid=(xs.shape[0] // TM_MOE,),
        in_specs=in_specs,
        out_specs=pl.BlockSpec((TM_MOE, D_MODEL), lambda i, te, nt: (first_tile + i, 0)),
    )
    return pl.pallas_call(
        functools.partial(_moe_kernel, first_tile),
        grid_spec=grid_spec,
        out_shape=jax.ShapeDtypeStruct((P_MOE, D_MODEL), BF16),
        input_output_aliases=aliases,
        compiler_params=_params("arbitrary"),
        name="moe_experts",
    )(*args)


def _moe(tile_expert, n_tiles, xn, row_token, wg, wu, wd):
    part_rows = P_MOE // MOE_PARTS
    ys = None
    for p in range(MOE_PARTS):
        xs = _take_rows(xn, row_token[p * part_rows:(p + 1) * part_rows])
        ys = _moe_part(tile_expert, n_tiles, xs, wg, wu, wd, ys, p * (NT_MOE // MOE_PARTS))
    return ys


def _take_rows(x, rows):
    return x.at[rows].get(mode="promise_in_bounds")


def _route(route, counts):
    counts = counts[0, :N_EXPERTS].astype(jnp.int32)
    tiles = (counts + TM_MOE - 1) // TM_MOE
    tile_end = jnp.cumsum(tiles)
    row_start = (tile_end - tiles) * TM_MOE

    def position(expert_lane, rank_lane):
        expert = route[:, expert_lane].astype(jnp.int32)
        start = sum(jnp.where(expert == e, row_start[e], 0) for e in range(N_EXPERTS))
        return start + route[:, rank_lane].astype(jnp.int32)

    pos1, pos2 = position(R_E1, R_RANK1), position(R_E2, R_RANK2)
    tile_expert = jnp.minimum(
        jnp.sum(jnp.arange(NT_MOE, dtype=jnp.int32)[:, None] >= tile_end[None, :], axis=1),
        N_EXPERTS - 1)
    pad = tiles * TM_MOE - counts
    pad_end = jnp.cumsum(pad)
    k = jnp.arange(P_MOE - TOP_K * T, dtype=jnp.int32)
    lanes = jnp.arange(N_EXPERTS, dtype=jnp.int32)[None, :]
    filler_expert = jnp.sum(k[:, None] >= pad_end[None, :], axis=1)
    filler_onehot = (filler_expert[:, None] == lanes).astype(jnp.int32)
    in_expert = jnp.sum(filler_onehot * (row_start + counts - (pad_end - pad))[None, :], axis=1) + k
    in_tail = tile_end[-1] * TM_MOE + (k - pad_end[-1])
    filler_pos = jnp.where(filler_expert < N_EXPERTS, in_expert, in_tail)
    token = jnp.arange(T, dtype=jnp.int32)
    _, row_token = lax.sort(
        (jnp.concatenate([pos1, pos2, filler_pos]),
         jnp.concatenate([token, token, jnp.zeros_like(k)])),
        num_keys=1)
    return (tile_expert.astype(jnp.int32), tile_end[-1:].astype(jnp.int32), row_token,
            pos1, pos2)


def _combine_kernel(last, h_ref, y1_ref, y2_ref, route_ref, g_ref, *out_refs):
    w1 = route_ref[:, R_W1:R_W1 + 1]
    w2 = route_ref[:, R_W2:R_W2 + 1]
    h = h_ref[...] + w1 * y1_ref[...].astype(F32) + w2 * y2_ref[...].astype(F32)
    if last:
        out_refs[0][...] = _rms(h, g_ref[...])
    else:
        out_refs[0][...] = h
        out_refs[1][...] = _rms(h, g_ref[...]).astype(BF16)


def _combine(h, y1, y2, route, g, last):
    row = pl.BlockSpec((TM_NORM, D_MODEL), lambda i: (i, 0))
    out_specs = [row] if last else [row, row]
    out_shape = [jax.ShapeDtypeStruct((T, D_MODEL), F32)]
    if not last:
        out_shape.append(jax.ShapeDtypeStruct((T, D_MODEL), BF16))
    return pl.pallas_call(
        functools.partial(_combine_kernel, last),
        grid=(T // TM_NORM,),
        in_specs=[row, row, row,
                  pl.BlockSpec((TM_NORM, ROUTE_LANES), lambda i: (i, 0)),
                  pl.BlockSpec((1, D_MODEL), lambda i: (0, 0))],
        out_specs=out_specs,
        out_shape=out_shape,
        compiler_params=_params("parallel"),
        name="combine_final" if last else "combine",
    )(h, y1, y2, route, g)


def kernel(x, mix_norm_g, w_in, b_gate, conv_dw_w, conv_dw_b, conv_ln_g, conv_ln_b,
           conv_out_w, sgu_ln_g, sgu_ln_b, sgu_ws, sgu_bs, sgu_out_w, attn_rel_bias,
           attn_out_w, w_out, ffn_norm_g, ffn_w_gate, ffn_w_up, ffn_w_down, moe_router,
           moe_w_gate, moe_w_up, moe_w_down, final_norm_g):
    h = x.reshape(T, D_MODEL)
    xn = _norm(h, mix_norm_g[0][None], BF16)
    moe_wg = moe_w_gate.astype(BF16).reshape(-1, D_MODEL, D_FF_EXPERT)
    moe_wu = moe_w_up.astype(BF16).reshape(-1, D_MODEL, D_FF_EXPERT)
    moe_wd = moe_w_down.astype(BF16).reshape(-1, D_FF_EXPERT, D_MODEL)
    for layer in range(DEPTH):
        last = layer == DEPTH - 1
        next_g = (final_norm_g if last else mix_norm_g[layer + 1])[None]
        w_in_l = jnp.concatenate(
            [w_in[layer][:, N_MIX:], w_in[layer][:, :N_MIX]], axis=1).astype(BF16)
        sgu_bias = jnp.repeat(sgu_bs[layer].T, SGU_GW, axis=1)
        gates, qkv, xc, xs = _mix_in(
            xn, w_in_l, conv_dw_w[layer], conv_dw_b[layer][None], conv_ln_g[layer][None],
            conv_ln_b[layer][None], sgu_ln_g[layer][None], sgu_ln_b[layer][None],
            sgu_ws[layer], sgu_bias)
        xa = _attn_branch(qkv, _attn_bias(attn_rel_bias[layer]))

        i = layer // 2
        outs = _merge(h, xc, xs, xa, gates, b_gate[layer][None],
                      conv_out_w[layer].astype(BF16), sgu_out_w[layer].astype(BF16),
                      attn_out_w[layer].astype(BF16), w_out[layer].astype(BF16),
                      ffn_norm_g[layer][None], moe_router[i] if layer % 2 == 1 else None)
        if layer % 2 == 0:
            h, xn2 = outs
            h, xn = _ffn(h, xn2, ffn_w_gate[i].astype(BF16), ffn_w_up[i].astype(BF16),
                         ffn_w_down[i].astype(BF16), next_g)
        else:
            h, xn2, route, counts = outs
            tile_expert, n_tiles, row_token, pos1, pos2 = _route(route, counts)
            ys = _moe(tile_expert + i * N_EXPERTS, n_tiles, xn2, row_token,
                      moe_wg, moe_wu, moe_wd)
            outs = _combine(h, _take_rows(ys, pos1), _take_rows(ys, pos2),
                            route, next_g, last)
            if last:
                h = outs[0]
            else:
                h, xn = outs
    return h.reshape(BATCH, SEQ, D_MODEL)
```
